```python
import math
import jax, jax.numpy as jnp
from jax import lax
import numpy as np

D_MODEL = 1024
BATCH = 8
SEQ = 4096
DEPTH = 1

RET_HEADS = 4
RET_QK_DIM = 64
RET_V_DIM = 128
RET_WIDTH = RET_HEADS * RET_V_DIM
RET_CHUNK = 128
ROPE_BASE = 10000.0
CONV_CH = 512
CONV_K = 31
IN_SPLITS = (RET_HEADS * RET_QK_DIM, RET_HEADS * RET_QK_DIM, RET_WIDTH, RET_WIDTH,
             CONV_CH, CONV_CH, D_MODEL, D_MODEL)
IN_COLS = sum(IN_SPLITS)
PEER_HEADS = 8
PEER_QDIM = 256
PEER_HALF = PEER_QDIM // 2
N_KEYS = 128
N_EXPERTS = N_KEYS * N_KEYS
PEER_TOPK = 16
TOKEN_BLOCK = 128

RMS_EPS = 1e-6
LN_EPS = 1e-5

kernel_name = "hybrid_retention_conformer_peer"


def rms_norm(x, g):
    xf = x.astype(jnp.float32)
    y = xf * lax.rsqrt(jnp.mean(xf * xf, axis=-1, keepdims=True) + RMS_EPS)
    return (y * g.astype(jnp.float32)).astype(x.dtype)


def layer_norm(x, g, b):
    xf = x.astype(jnp.float32)
    mu = jnp.mean(xf, axis=-1, keepdims=True)
    var = jnp.mean(jnp.square(xf - mu), axis=-1, keepdims=True)
    y = (xf - mu) * lax.rsqrt(var + LN_EPS)
    return (y * g.astype(jnp.float32) + b.astype(jnp.float32)).astype(x.dtype)


def rotary(t):
    s, d = t.shape[1], t.shape[-1]
    half = d // 2
    inv = 1.0 / (ROPE_BASE ** (jnp.arange(half, dtype=jnp.float32) / half))
    ang = jnp.arange(s, dtype=jnp.float32)[:, None] * inv[None, :]
    cos = jnp.cos(ang)[None, :, None, :]
    sin = jnp.sin(ang)[None, :, None, :]
    tf = t.astype(jnp.float32)
    t1, t2 = tf[..., :half], tf[..., half:]
    return jnp.concatenate([t1 * cos - t2 * sin, t1 * sin + t2 * cos], axis=-1)


def chunkwise_retention(q, k, v):
    bsz, s, h, dk = q.shape
    dv = v.shape[-1]
    n_chunks = s // RET_CHUNK
    c = RET_CHUNK
    log_gamma = jnp.log1p(-(2.0 ** (-5.0 - jnp.arange(h, dtype=jnp.float32))))
    n_idx = jnp.arange(c, dtype=jnp.float32)
    diff = n_idx[:, None] - n_idx[None, :]
    intra = jnp.where(diff[None] >= 0, jnp.exp(log_gamma[:, None, None] * jnp.maximum(diff, 0.0)[None]), 0.0)
    q_decay = jnp.exp(log_gamma[:, None] * (n_idx + 1.0)[None])
    k_decay = jnp.exp(log_gamma[:, None] * (c - 1.0 - n_idx)[None])
    chunk_decay = jnp.exp(log_gamma * c)

    def to_chunks(t):
        return t.reshape(bsz, n_chunks, c, h, t.shape[-1]).transpose(1, 0, 3, 2, 4)

    qc, kc, vc = to_chunks(q), to_chunks(k), to_chunks(v)

    def step(state, inp):
        qi, ki, vi = inp
        scores = jnp.einsum('bhnd,bhmd->bhnm', qi, ki) * intra[None]
        inner = jnp.einsum('bhnm,bhme->bhne', scores, vi)
        cross = jnp.einsum('bhnd,bhde->bhne', qi, state) * q_decay[None, :, :, None]
        new_state = state * chunk_decay[None, :, None, None] + jnp.einsum(
            'bhmd,bhme->bhde', ki * k_decay[None, :, :, None], vi)
        return new_state, inner + cross

    state0 = jnp.zeros((bsz, h, dk, dv), jnp.float32)
    _, out = lax.scan(step, state0, (qc, kc, vc))
    return out.transpose(1, 0, 3, 2, 4).reshape(bsz, s, h, dv)


def retention_branch(q, k, v, g, gn_g, w_proj):
    bsz, s, _ = q.shape
    dt = q.dtype
    q = rotary(q.reshape(bsz, s, RET_HEADS, RET_QK_DIM))
    k = rotary(k.reshape(bsz, s, RET_HEADS, RET_QK_DIM)) * (RET_QK_DIM ** -0.5)
    v = v.reshape(bsz, s, RET_HEADS, RET_V_DIM).astype(jnp.float32)
    o = chunkwise_retention(q, k, v)
    mu = jnp.mean(o, axis=-1, keepdims=True)
    var = jnp.mean(jnp.square(o - mu), axis=-1, keepdims=True)
    o = (o - mu) * lax.rsqrt(var + LN_EPS) * gn_g.astype(jnp.float32)[None, None]
    o = o.reshape(bsz, s, RET_WIDTH).astype(dt)
    return (jax.nn.silu(g) * o) @ w_proj


def conv_branch(u_val, u_gate, dw_w, dw_b, ln_g, ln_b, w_proj, b_proj):
    c = u_val * jax.nn.sigmoid(u_gate)
    c = lax.conv_general_dilated(
        c, dw_w[:, None, :].astype(c.dtype), window_strides=(1,),
        padding=[(CONV_K - 1, 0)], dimension_numbers=('NWC', 'WIO', 'NWC'),
        feature_group_count=CONV_CH) + dw_b
    c = jax.nn.silu(layer_norm(c, ln_g, ln_b))
    return c @ w_proj + b_proj


def peer_ffn(hn, w_query, sub_keys, u_tab, v_tab):
    bsz, s, d = hn.shape
    t = bsz * s
    hf = hn.reshape(t, d)
    q = (hf @ w_query).reshape(t, PEER_HEADS, 2, PEER_HALF)
    scores = jnp.einsum('thpd,hpnd->thpn', q, sub_keys).astype(jnp.float32)
    s1, i1 = lax.top_k(scores[:, :, 0], PEER_TOPK)
    s2, i2 = lax.top_k(scores[:, :, 1], PEER_TOPK)
    cand = (s1[..., :, None] + s2[..., None, :]).reshape(t, PEER_HEADS, PEER_TOPK * PEER_TOPK)
    best, pos = lax.top_k(cand, PEER_TOPK)
    e1 = jnp.take_along_axis(i1, pos // PEER_TOPK, axis=-1)
    e2 = jnp.take_along_axis(i2, pos % PEER_TOPK, axis=-1)
    experts = (e1 * N_KEYS + e2).reshape(t, PEER_HEADS * PEER_TOPK)
    gates = jax.nn.softmax(best, axis=-1).reshape(t, PEER_HEADS * PEER_TOPK).astype(hn.dtype)

    nb = t // TOKEN_BLOCK
    kk = PEER_HEADS * PEER_TOPK

    def apply_block(inp):
        hb, eb, gb = inp
        u = u_tab[eb]
        a = jax.nn.gelu(jnp.einsum('td,tkd->tk', hb, u), approximate=False) * gb
        return jnp.einsum('tk,tkd->td', a, v_tab[eb])

    out = lax.map(apply_block, (hf.reshape(nb, TOKEN_BLOCK, d),
                                experts.reshape(nb, TOKEN_BLOCK, kk),
                                gates.reshape(nb, TOKEN_BLOCK, kk)))
    return out.reshape(bsz, s, d)


def setup_inputs(seed: int = 0) -> dict:
    key = jax.random.key(seed)
    ks = jax.random.split(key, 20)
    nrm = jax.random.normal
    f32 = jnp.float32
    return {
        "x": nrm(ks[0], (BATCH, SEQ, D_MODEL), f32),
        "rms_mix_g": 1.0 + 0.02 * nrm(ks[1], (DEPTH, D_MODEL), f32),
        "w_in": nrm(ks[2], (DEPTH, D_MODEL, IN_COLS), f32) * D_MODEL ** -0.5,
        "ret_gn_g": 1.0 + 0.02 * nrm(ks[3], (DEPTH, RET_HEADS, RET_V_DIM), f32),
        "conv_dw_w": nrm(ks[4], (DEPTH, CONV_K, CONV_CH), f32) * CONV_K ** -0.5,
        "conv_dw_b": 0.02 * nrm(ks[5], (DEPTH, CONV_CH), f32),
        "conv_ln_g": 1.0 + 0.02 * nrm(ks[6], (DEPTH, CONV_CH), f32),
        "conv_ln_b": 0.02 * nrm(ks[7], (DEPTH, CONV_CH), f32),
        "w_ret_proj": nrm(ks[8], (DEPTH, RET_WIDTH, D_MODEL), f32) * RET_WIDTH ** -0.5,
        "w_conv_proj": nrm(ks[9], (DEPTH, CONV_CH, D_MODEL), f32) * CONV_CH ** -0.5,
        "b_conv_proj": 0.02 * nrm(ks[10], (DEPTH, D_MODEL), f32),
        "w_out": nrm(ks[11], (DEPTH, D_MODEL, D_MODEL), f32) * D_MODEL ** -0.5,
        "rms_ffn_g": 1.0 + 0.02 * nrm(ks[12], (DEPTH, D_MODEL), f32),
        "w_query": nrm(ks[13], (DEPTH, D_MODEL, PEER_HEADS * PEER_QDIM), f32) * D_MODEL ** -0.5,
        "peer_sub_keys": nrm(ks[14], (DEPTH, PEER_HEADS, 2, N_KEYS, PEER_HALF), f32) * PEER_HALF ** -0.5,
        "peer_u": nrm(ks[15], (DEPTH, N_EXPERTS, D_MODEL), f32) * D_MODEL ** -0.5,
        "peer_v": nrm(ks[16], (DEPTH, N_EXPERTS, D_MODEL), f32) * 0.5,
        "rms_final_g": 1.0 + 0.02 * nrm(ks[17], (D_MODEL,), f32),
    }


def reference(x, rms_mix_g, w_in, ret_gn_g, conv_dw_w, conv_dw_b, conv_ln_g, conv_ln_b,
              w_ret_proj, w_conv_proj, b_conv_proj, w_out, rms_ffn_g, w_query,
              peer_sub_keys, peer_u, peer_v, rms_final_g):
    split_points = list(np.cumsum(IN_SPLITS)[:-1])
    for l in range(DEPTH):
        h = rms_norm(x, rms_mix_g[l])
        proj = h @ w_in[l]
        q, k, v, g_ret, u_val, u_gate, gate_a, gate_b = jnp.split(proj, split_points, axis=-1)
        a = retention_branch(q, k, v, g_ret, ret_gn_g[l], w_ret_proj[l])
        b = conv_branch(u_val, u_gate, conv_dw_w[l], conv_dw_b[l], conv_ln_g[l], conv_ln_b[l],
                        w_conv_proj[l], b_conv_proj[l])
        mixed = jax.nn.sigmoid(gate_a) * a + jax.nn.sigmoid(gate_b) * b
        x = x + mixed @ w_out[l]
        hn = rms_norm(x, rms_ffn_g[l])
        x = x + peer_ffn(hn, w_query[l], peer_sub_keys[l], peer_u[l], peer_v[l])
    return rms_norm(x, rms_final_g)
```

```python
import functools

import numpy as np
import jax
import jax.numpy as jnp
from jax import lax
from jax.experimental import pallas as pl
from jax.experimental.pallas import tpu as pltpu

F32 = jnp.float32
BF16 = jnp.bfloat16

RET_HEADS = 4
RET_QK_DIM = 64
RET_V_DIM = 128
RET_QK = RET_HEADS * RET_QK_DIM
RET_WIDTH = RET_HEADS * RET_V_DIM
ROPE_BASE = 10000.0
CONV_CH = 512
CONV_K = 31
PEER_HEADS = 8
PEER_HALF = 128
N_KEYS = 128
PEER_TOPK = 16
RMS_EPS = 1e-6
LN_EPS = 1e-5
INV_SQRT2 = 0.7071067811865476

LANES = 128
SUBLANES = 8
VMEM_LIMIT_BYTES = 56 * 1024 * 1024
CONV_HIST = 32
CONV_ROWS = 64
NOT_SELECTED = 127.0


def _const_spec(shape):
    nd = len(shape)
    return pl.BlockSpec(shape, lambda *_: (0,) * nd, pipeline_mode=pl.Buffered(1))


def _swap_halves(t):
    half = RET_QK_DIM // 2
    parts = []
    for c in range(t.shape[1] // LANES):
        tc = t[:, c * LANES:(c + 1) * LANES]
        lane = lax.broadcasted_iota(jnp.int32, tc.shape, 1)
        first = (lane % RET_QK_DIM) < half
        nxt = pltpu.roll(tc, LANES - half, 1)
        prv = pltpu.roll(tc, half, 1)
        parts.append(jnp.where(first, nxt, prv))
    return jnp.concatenate(parts, axis=1)


def _mixer_kernel(x_ref, g_ref, win_ref, cos_ref, sin_ref, qd_ref, kd_ref, intra_ref, cd_ref, bd_ref,
                  hm_ref, gn_ref, dww_ref, dwb_ref, lng_ref, lnb_ref, wret_ref, wconv_ref, bconv_ref,
                  wout_ref, o_ref, state_ref, cbuf_ref, conv_ref, *, ts):
    @pl.when(pl.program_id(1) == 0)
    def _():
        state_ref[...] = jnp.zeros_like(state_ref)
        cbuf_ref[0:CONV_HIST, :] = jnp.zeros((CONV_HIST, CONV_CH), F32)

    x = x_ref[...]
    h = (x * lax.rsqrt(jnp.mean(x * x, axis=-1, keepdims=True) + RMS_EPS) * g_ref[...]).astype(BF16)

    c0 = 2 * RET_QK + RET_WIDTH
    qkv = jnp.dot(h, win_ref[:, 0:c0], preferred_element_type=F32)
    q = qkv[:, 0:RET_QK]
    k = qkv[:, RET_QK:2 * RET_QK]
    vb = qkv[:, 2 * RET_QK:c0].astype(BF16)
    cos = cos_ref[...]
    sin = sin_ref[...]
    qr = q * cos + _swap_halves(q) * sin
    kr = (k * cos + _swap_halves(k) * sin) * (RET_QK_DIM ** -0.5)
    qb = qr.astype(BF16)
    kb = kr.astype(BF16)
    inner = []
    for hh in range(RET_HEADS):
        qm = (qr * hm_ref[hh:hh + 1, :]).astype(BF16)
        sc = lax.dot_general(qm, kb, (((1,), (1,)), ((), ())), preferred_element_type=F32)
        sc = sc * intra_ref[hh]
        inner.append(jnp.dot(sc.astype(BF16), vb[:, hh * RET_V_DIM:(hh + 1) * RET_V_DIM],
                             preferred_element_type=F32))
    st = state_ref[...]
    cross = jnp.dot(qb, st.astype(BF16), preferred_element_type=F32) * qd_ref[...]
    o = jnp.concatenate(inner, axis=1) + cross
    kv = lax.dot_general((kr * kd_ref[...]).astype(BF16), vb, (((0,), (0,)), ((), ())),
                         preferred_element_type=F32)
    state_ref[...] = st * cd_ref[...] + kv * bd_ref[...]

    normed = []
    for hh in range(RET_HEADS):
        oh = o[:, hh * RET_V_DIM:(hh + 1) * RET_V_DIM]
        d = oh - jnp.mean(oh, axis=-1, keepdims=True)
        normed.append(d * lax.rsqrt(jnp.mean(d * d, axis=-1, keepdims=True) + LN_EPS))
    on = jnp.concatenate(normed, axis=1) * gn_ref[...]

    c1 = c0 + RET_WIDTH + 2 * CONV_CH
    p2 = jnp.dot(h, win_ref[:, c0:c1], preferred_element_type=F32)
    g_ret = p2[:, 0:RET_WIDTH]
    yret = (g_ret * jax.nn.sigmoid(g_ret) * on).astype(BF16)
    a_out = jnp.dot(yret, wret_ref[...], preferred_element_type=F32)

    u_val = p2[:, RET_WIDTH:RET_WIDTH + CONV_CH]
    u_gate = p2[:, RET_WIDTH + CONV_CH:RET_WIDTH + 2 * CONV_CH]
    cbuf_ref[CONV_HIST:CONV_HIST + ts, :] = u_val * jax.nn.sigmoid(u_gate)
    base = CONV_HIST - (CONV_K - 1)
    for rc in range(ts // CONV_ROWS):
        acc = jnp.broadcast_to(dwb_ref[...], (CONV_ROWS, CONV_CH))
        for jj in range(CONV_K):
            acc = acc + dww_ref[jj:jj + 1, :] * cbuf_ref[pl.ds(rc * CONV_ROWS + base + jj, CONV_ROWS), :]
        conv_ref[rc * CONV_ROWS:(rc + 1) * CONV_ROWS, :] = acc
    cbuf_ref[0:CONV_HIST, :] = cbuf_ref[ts:ts + CONV_HIST, :]
    cv = conv_ref[...]
    d = cv - jnp.mean(cv, axis=-1, keepdims=True)
    ln = d * lax.rsqrt(jnp.mean(d * d, axis=-1, keepdims=True) + LN_EPS) * lng_ref[...] + lnb_ref[...]
    yconv = (ln * jax.nn.sigmoid(ln)).astype(BF16)
    b_out = jnp.dot(yconv, wconv_ref[...], preferred_element_type=F32) + bconv_ref[...]

    p3 = jnp.dot(h, win_ref[:, c1:], preferred_element_type=F32)
    dm = x.shape[1]
    mixed = jax.nn.sigmoid(p3[:, 0:dm]) * a_out + jax.nn.sigmoid(p3[:, dm:2 * dm]) * b_out
    o_ref[...] = x + jnp.dot(mixed.astype(BF16), wout_ref[...], preferred_element_type=F32)


def _mixer_tables(seq, ts):
    half = RET_QK_DIM // 2
    inv = 1.0 / (ROPE_BASE ** (np.arange(half, dtype=np.float64) / half))
    ang = np.arange(seq, dtype=np.float64)[:, None] * inv[None, :]
    cos = np.tile(np.concatenate([np.cos(ang), np.cos(ang)], axis=1), (1, RET_HEADS))
    sin = np.tile(np.concatenate([-np.sin(ang), np.sin(ang)], axis=1), (1, RET_HEADS))
    log_gamma = np.log1p(-(2.0 ** (-5.0 - np.arange(RET_HEADS, dtype=np.float64))))
    n = np.arange(ts, dtype=np.float64)
    diff = n[:, None] - n[None, :]
    intra = np.where(diff[None] >= 0, np.exp(log_gamma[:, None, None] * np.maximum(diff, 0.0)[None]), 0.0)
    q_decay = np.exp(log_gamma[:, None] * (n + 1.0)[None])
    k_decay = np.exp(log_gamma[:, None] * (ts - 1.0 - n)[None])
    chunk_decay = np.exp(log_gamma * ts)
    qd = np.repeat(q_decay.T, RET_V_DIM, axis=1)
    kd = np.repeat(k_decay.T, RET_QK_DIM, axis=1)
    cd = np.repeat(chunk_decay, RET_V_DIM)[None, :]
    row_head = np.arange(RET_QK) // RET_QK_DIM
    col_head = np.arange(RET_WIDTH) // RET_V_DIM
    bd = (row_head[:, None] == col_head[None, :]).astype(np.float64)
    hm = (np.arange(RET_HEADS)[:, None] == row_head[None, :]).astype(np.float64)
    as32 = lambda a: jnp.asarray(a, dtype=F32)
    return tuple(as32(a) for a in (cos, sin, qd, kd, intra, cd, bd, hm))


def _mixer_tile(seq):
    for ts in (256, 128, 64):
        if seq % ts == 0:
            return ts
    raise ValueError(f"sequence length {seq} must be a multiple of {CONV_ROWS}")


def _mixer(x, g, w_in, gn_g, dw_w, dw_b, ln_g, ln_b, w_ret, w_conv, b_conv, w_out):
    bsz, seq, dm = x.shape
    ts = _mixer_tile(seq)
    cos, sin, qd, kd, intra, cd, bd, hm = _mixer_tables(seq, ts)
    row = lambda a: a.reshape(1, -1).astype(F32)
    in_cols = w_in.shape[1]
    grid = (bsz, seq // ts)
    in_specs = [
        pl.BlockSpec((None, ts, dm), lambda b, j: (b, j, 0)),
        _const_spec((1, dm)),
        _const_spec((dm, in_cols)),
        pl.BlockSpec((ts, RET_QK), lambda b, j: (j, 0)),
        pl.BlockSpec((ts, RET_QK), lambda b, j: (j, 0)),
        _const_spec((ts, RET_WIDTH)),
        _const_spec((ts, RET_QK)),
        _const_spec((RET_HEADS, ts, ts)),
        _const_spec((1, RET_WIDTH)),
        _const_spec((RET_QK, RET_WIDTH)),
        _const_spec((RET_HEADS, RET_QK)),
        _const_spec((1, RET_WIDTH)),
        _const_spec((CONV_K, CONV_CH)),
        _const_spec((1, CONV_CH)),
        _const_spec((1, CONV_CH)),
        _const_spec((1, CONV_CH)),
        _const_spec((RET_WIDTH, dm)),
        _const_spec((CONV_CH, dm)),
        _const_spec((1, dm)),
        _const_spec((dm, dm)),
    ]
    return pl.pallas_call(
        functools.partial(_mixer_kernel, ts=ts),
        name="mixer",
        grid=grid,
        in_specs=in_specs,
        out_specs=pl.BlockSpec((None, ts, dm), lambda b, j: (b, j, 0)),
        out_shape=jax.ShapeDtypeStruct((bsz, seq, dm), F32),
        scratch_shapes=[
            pltpu.VMEM((RET_QK, RET_WIDTH), F32),
            pltpu.VMEM((CONV_HIST + ts, CONV_CH), F32),
            pltpu.VMEM((ts, CONV_CH), F32),
        ],
        compiler_params=pltpu.CompilerParams(
            dimension_semantics=("arbitrary", "arbitrary"), vmem_limit_bytes=VMEM_LIMIT_BYTES),
    )(x, row(g), w_in.astype(BF16), cos, sin, qd, kd, intra, cd, bd, hm, row(gn_g), dw_w.astype(F32),
      row(dw_b), row(ln_g), row(ln_b), w_ret.astype(BF16), w_conv.astype(BF16), row(b_conv),
      w_out.astype(BF16))


def _top16(s, row_id):
    cur = s
    rank = jnp.full(s.shape, NOT_SELECTED, F32)
    vals = []
    for r in range(PEER_TOPK):
        m = jnp.max(cur, axis=0, keepdims=True)
        first = jnp.min(jnp.where(cur == m, row_id, 1e9), axis=0, keepdims=True)
        hit = row_id == first
        rank = jnp.where(hit, float(r), rank)
        cur = jnp.where(hit, -jnp.inf, cur)
        vals.append(m)
    return jnp.concatenate(vals, axis=0), rank


def _staircase(v1, v2, e1x, e2x):
    k = PEER_TOPK
    row = lax.broadcasted_iota(jnp.int32, (SUBLANES, LANES), 0).astype(F32)
    ninf = jnp.full((SUBLANES, LANES), -jnp.inf, F32)
    cand, pos, wgt = [], [], []
    for b0 in (0, SUBLANES):
        cand.append(v1[0:1] + v2[b0:b0 + SUBLANES]); pos.append(row + float(b0))
        wgt.append(e1x[0:1] * e2x[b0:b0 + SUBLANES])
    cand.append(v1[1:2] + v2[0:SUBLANES]); pos.append(row + float(k)); wgt.append(e1x[1:2] * e2x[0:SUBLANES])
    cand.append(v1[SUBLANES:k] + v2[0:1]); pos.append((row + float(SUBLANES)) * float(k))
    wgt.append(e1x[SUBLANES:k] * e2x[0:1])
    n_col = 5
    for b in range(n_col):
        a_max = float(k // (b + 1) - 1)
        ok = (row >= 2.0) & (row <= a_max)
        cand.append(jnp.where(ok, v1[0:SUBLANES] + v2[b:b + 1], ninf))
        pos.append(jnp.where(ok, row * float(k) + float(b), -1.0))
        wgt.append(e1x[0:SUBLANES] * e2x[b:b + 1])
    ng = len(cand)
    sel = [jnp.zeros((SUBLANES, LANES), F32) for _ in range(ng)]
    for _ in range(k):
        m = cand[0]
        for g in range(1, ng):
            m = jnp.maximum(m, cand[g])
        m = jnp.max(m, axis=0, keepdims=True)
        fp = jnp.where(cand[0] == m, pos[0], 1e9)
        for g in range(1, ng):
            fp = jnp.minimum(fp, jnp.where(cand[g] == m, pos[g], 1e9))
        fp = jnp.min(fp, axis=0, keepdims=True)
        for g in range(ng):
            hit = pos[g] == fp
            sel[g] = jnp.where(hit, 1.0, sel[g])
            cand[g] = jnp.where(hit, -jnp.inf, cand[g])
    b0cnt = jnp.sum(sel[0] + sel[1], axis=0, keepdims=True)
    b1cnt = jnp.sum(sel[2], axis=0, keepdims=True)
    low = sel[4]
    for g in range(5, ng):
        low = low + sel[g]
    low = low + jnp.where(row == 0.0, b0cnt, 0.0) + jnp.where(row == 1.0, b1cnt, 0.0)
    counts = jnp.concatenate([low, sel[3]], axis=0)
    zt = sel[0] * wgt[0]
    for g in range(1, ng):
        zt = zt + sel[g] * wgt[g]
    return counts, jnp.sum(zt, axis=0, keepdims=True)


def _peer_kernel(x_ref, g_ref, wq_ref, keys_ref, u_ref, vt_ref, gfin_ref, o_ref,
                 hn_ref, s_ref, rank2_ref, w2_ref, bcnt_ref, w1_ref, st_ref, at_ref, acc_ref,
                 *, tt, eb, final_norm):
    j = pl.program_id(1)
    nlb = tt // LANES
    qd = 2 * PEER_HALF

    @pl.when(j == 0)
    def _routing():
        x = x_ref[...]
        hn_ref[...] = (x * lax.rsqrt(jnp.mean(x * x, axis=-1, keepdims=True) + RMS_EPS)
                       * g_ref[...]).astype(BF16)
        acc_ref[...] = jnp.zeros_like(acc_ref)
        hn = hn_ref[...]
        for hh in range(PEER_HEADS):
            qt = lax.dot_general(wq_ref[hh * qd:(hh + 1) * qd, :], hn, (((1,), (1,)), ((), ())),
                                 preferred_element_type=F32).astype(BF16)
            for p in range(2):
                sc = jnp.dot(keys_ref[2 * hh + p], qt[p * PEER_HALF:(p + 1) * PEER_HALF, :],
                             preferred_element_type=F32)
                for lb in range(nlb):
                    s_ref[2 * hh + p, lb] = sc[:, lb * LANES:(lb + 1) * LANES]

        row_id = lax.broadcasted_iota(jnp.int32, (N_KEYS, LANES), 0).astype(F32)

        def route(i, carry):
            hh = i // nlb
            lb = i % nlb
            s1 = s_ref[2 * hh, lb]
            s2 = s_ref[2 * hh + 1, lb]
            v1, rank1 = _top16(s1, row_id)
            v2, rank2 = _top16(s2, row_id)
            e1x = jnp.exp(v1 - v1[0:1])
            e2x = jnp.exp(v2 - v2[0:1])
            counts, z = _staircase(v1, v2, e1x, e2x)
            bcnt = jnp.zeros((N_KEYS, LANES), F32)
            for a in range(PEER_TOPK):
                bcnt = jnp.where(rank1 == float(a), counts[a:a + 1], bcnt)
            bcnt_ref[hh, lb] = bcnt
            w1_ref[hh, lb] = jnp.exp(s1 - v1[0:1]) * (0.5 / z)
            w2_ref[hh, lb] = jnp.exp(s2 - v2[0:1]).astype(BF16)
            rank2_ref[hh, lb] = rank2.astype(BF16)
            return carry

        lax.fori_loop(0, PEER_HEADS * nlb, route, 0)

    st_ref[...] = lax.dot_general(u_ref[...], hn_ref[...], (((1,), (1,)), ((), ())),
                                  preferred_element_type=F32)

    def build(e, carry):
        e1 = j * eb + e
        r0 = pl.multiple_of(e * N_KEYS, N_KEYS)
        for lb in range(nlb):
            xs = st_ref[pl.ds(r0, N_KEYS), lb * LANES:(lb + 1) * LANES]
            ge = xs * (1.0 + lax.erf(xs * INV_SQRT2))
            gate = jnp.zeros((N_KEYS, LANES), BF16)
            for hh in range(PEER_HEADS):
                cnt = bcnt_ref[hh, lb, pl.ds(e1, 1), :].astype(BF16)
                w1 = w1_ref[hh, lb, pl.ds(e1, 1), :].astype(BF16)
                w2 = w2_ref[hh, lb]
                gate = gate + jnp.where(rank2_ref[hh, lb] < cnt, w2, jnp.zeros_like(w2)) * w1
            at_ref[pl.ds(r0, N_KEYS), lb * LANES:(lb + 1) * LANES] = ge.astype(BF16) * gate
        return carry

    lax.fori_loop(0, eb, build, 0)
    acc_ref[...] += jnp.dot(vt_ref[...], at_ref[...], preferred_element_type=F32)

    @pl.when(j == pl.num_programs(1) - 1)
    def _finish():
        y = acc_ref[...].T + x_ref[...]
        if final_norm:
            y = y * lax.rsqrt(jnp.mean(y * y, axis=-1, keepdims=True) + RMS_EPS) * gfin_ref[...]
        o_ref[...] = y


def _peer_tiles(tokens):
    for tt in (512, 256, 128):
        if tokens % tt == 0:
            return tt, 8
    raise ValueError(f"token count {tokens} must be a multiple of {LANES}")


def _peer(x, g, w_query, sub_keys, u_tab, v_tab, g_final, final_norm):
    tokens, dm = x.shape
    tt, eb = _peer_tiles(tokens)
    nlb = tt // LANES
    eblk = eb * N_KEYS
    n_exp = u_tab.shape[0]
    assert n_exp == N_KEYS * N_KEYS and N_KEYS % eb == 0
    assert sub_keys.shape == (PEER_HEADS, 2, N_KEYS, PEER_HALF)
    wq_t = w_query.T.astype(BF16)
    keys = sub_keys.reshape(PEER_HEADS * 2, N_KEYS, PEER_HALF).astype(BF16)
    u_bf = u_tab.astype(BF16)
    vt_bf = v_tab.T.astype(BF16)
    row = lambda a: a.reshape(1, -1).astype(F32)
    grid = (tokens // tt, N_KEYS // eb)
    return pl.pallas_call(
        functools.partial(_peer_kernel, tt=tt, eb=eb, final_norm=final_norm),
        name="peer",
        grid=grid,
        in_specs=[
            pl.BlockSpec((tt, dm), lambda i, j: (i, 0)),
            _const_spec((1, dm)),
            _const_spec(wq_t.shape),
            _const_spec(keys.shape),
            pl.BlockSpec((eblk, dm), lambda i, j: (j, 0)),
            pl.BlockSpec((dm, eblk), lambda i, j: (0, j)),
            _const_spec((1, dm)),
        ],
        out_specs=pl.BlockSpec((tt, dm), lambda i, j: (i, 0)),
        out_shape=jax.ShapeDtypeStruct((tokens, dm), F32),
        scratch_shapes=[
            pltpu.VMEM((tt, dm), BF16),
            pltpu.VMEM((2 * PEER_HEADS, nlb, N_KEYS, LANES), F32),
            pltpu.VMEM((PEER_HEADS, nlb, N_KEYS, LANES), BF16),
            pltpu.VMEM((PEER_HEADS, nlb, N_KEYS, LANES), BF16),
            pltpu.VMEM((PEER_HEADS, nlb, N_KEYS, LANES), F32),
            pltpu.VMEM((PEER_HEADS, nlb, N_KEYS, LANES), F32),
            pltpu.VMEM((eblk, tt), F32),
            pltpu.VMEM((eblk, tt), BF16),
            pltpu.VMEM((dm, tt), F32),
        ],
        compiler_params=pltpu.CompilerParams(
            dimension_semantics=("arbitrary", "arbitrary"), vmem_limit_bytes=VMEM_LIMIT_BYTES),
    )(x, row(g), wq_t, keys, u_bf, vt_bf, row(g_final))


def kernel(x, rms_mix_g, w_in, ret_gn_g, conv_dw_w, conv_dw_b, conv_ln_g, conv_ln_b, w_ret_proj, w_conv_proj,
           b_conv_proj, w_out, rms_ffn_g, w_query, peer_sub_keys, peer_u, peer_v, rms_final_g):
    bsz, seq, dm = x.shape
    depth = w_in.shape[0]
    for l in range(depth):
        x = _mixer(x, rms_mix_g[l], w_in[l], ret_gn_g[l], conv_dw_w[l], conv_dw_b[l], conv_ln_g[l],
                   conv_ln_b[l], w_ret_proj[l], w_conv_proj[l], b_conv_proj[l], w_out[l])
        x = _peer(x.reshape(bsz * seq, dm), rms_ffn_g[l], w_query[l], peer_sub_keys[l], peer_u[l], peer_v[l],
                  rms_final_g, final_norm=(l == depth - 1)).reshape(bsz, seq, dm)
    return x
```

```python
import functools

import numpy as np
import jax
import jax.numpy as jnp
from jax import lax
from jax.experimental import pallas as pl
from jax.experimental.pallas import tpu as pltpu

F32 = jnp.float32
BF16 = jnp.bfloat16

RET_HEADS = 4
RET_QK_DIM = 64
RET_V_DIM = 128
RET_QK = RET_HEADS * RET_QK_DIM
RET_WIDTH = RET_HEADS * RET_V_DIM
ROPE_BASE = 10000.0
CONV_CH = 512
CONV_K = 31
PEER_HEADS = 8
PEER_HALF = 128
N_KEYS = 128
PEER_TOPK = 16
RMS_EPS = 1e-6
LN_EPS = 1e-5
INV_SQRT2 = 0.7071067811865476

LANES = 128
SUBLANES = 8
VMEM_LIMIT_BYTES = 56 * 1024 * 1024
CONV_HIST = 32
CONV_ROWS = 64
PEER_CHUNK = 256
BUILD_ROWS = 32
NOT_SELECTED = 127.0


def _const_spec(shape):
    nd = len(shape)
    return pl.BlockSpec(shape, lambda *_: (0,) * nd, pipeline_mode=pl.Buffered(1))


def _swap_halves(t):
    half = RET_QK_DIM // 2
    parts = []
    for c in range(t.shape[1] // LANES):
        tc = t[:, c * LANES:(c + 1) * LANES]
        lane = lax.broadcasted_iota(jnp.int32, tc.shape, 1)
        first = (lane % RET_QK_DIM) < half
        nxt = pltpu.roll(tc, LANES - half, 1)
        prv = pltpu.roll(tc, half, 1)
        parts.append(jnp.where(first, nxt, prv))
    return jnp.concatenate(parts, axis=1)


def _mixer_kernel(x_ref, g_ref, win_ref, cos_ref, sin_ref, qd_ref, kd_ref, intra_ref, cd_ref, bd_ref,
                  hm_ref, gn_ref, dww_ref, dwb_ref, lng_ref, lnb_ref, wret_ref, wconv_ref, bconv_ref,
                  wout_ref, o_ref, state_ref, cbuf_ref, conv_ref, *, ts):
    @pl.when(pl.program_id(1) == 0)
    def _():
        state_ref[...] = jnp.zeros_like(state_ref)
        cbuf_ref[0:CONV_HIST, :] = jnp.zeros((CONV_HIST, CONV_CH), F32)

    x = x_ref[...]
    h = (x * lax.rsqrt(jnp.mean(x * x, axis=-1, keepdims=True) + RMS_EPS) * g_ref[...]).astype(BF16)

    c0 = 2 * RET_QK + RET_WIDTH
    qkv = jnp.dot(h, win_ref[:, 0:c0], preferred_element_type=F32)
    q = qkv[:, 0:RET_QK]
    k = qkv[:, RET_QK:2 * RET_QK]
    vb = qkv[:, 2 * RET_QK:c0].astype(BF16)
    cos = cos_ref[...]
    sin = sin_ref[...]
    qr = q * cos + _swap_halves(q) * sin
    kr = (k * cos + _swap_halves(k) * sin) * (RET_QK_DIM ** -0.5)
    qb = qr.astype(BF16)
    kb = kr.astype(BF16)
    inner = []
    for hh in range(RET_HEADS):
        qm = (qr * hm_ref[hh:hh + 1, :]).astype(BF16)
        sc = lax.dot_general(qm, kb, (((1,), (1,)), ((), ())), preferred_element_type=F32)
        sc = sc * intra_ref[hh]
        inner.append(jnp.dot(sc.astype(BF16), vb[:, hh * RET_V_DIM:(hh + 1) * RET_V_DIM],
                             preferred_element_type=F32))
    st = state_ref[...]
    cross = jnp.dot(qb, st.astype(BF16), preferred_element_type=F32) * qd_ref[...]
    o = jnp.concatenate(inner, axis=1) + cross
    kv = lax.dot_general((kr * kd_ref[...]).astype(BF16), vb, (((0,), (0,)), ((), ())),
                         preferred_element_type=F32)
    state_ref[...] = st * cd_ref[...] + kv * bd_ref[...]

    normed = []
    for hh in range(RET_HEADS):
        oh = o[:, hh * RET_V_DIM:(hh + 1) * RET_V_DIM]
        d = oh - jnp.mean(oh, axis=-1, keepdims=True)
        normed.append(d * lax.rsqrt(jnp.mean(d * d, axis=-1, keepdims=True) + LN_EPS))
    on = jnp.concatenate(normed, axis=1) * gn_ref[...]

    c1 = c0 + RET_WIDTH + 2 * CONV_CH
    p2 = jnp.dot(h, win_ref[:, c0:c1], preferred_element_type=F32)
    g_ret = p2[:, 0:RET_WIDTH]
    yret = (g_ret * jax.nn.sigmoid(g_ret) * on).astype(BF16)
    a_out = jnp.dot(yret, wret_ref[...], preferred_element_type=F32)

    u_val = p2[:, RET_WIDTH:RET_WIDTH + CONV_CH]
    u_gate = p2[:, RET_WIDTH + CONV_CH:RET_WIDTH + 2 * CONV_CH]
    cbuf_ref[CONV_HIST:CONV_HIST + ts, :] = u_val * jax.nn.sigmoid(u_gate)
    base = CONV_HIST - (CONV_K - 1)
    for rc in range(ts // CONV_ROWS):
        acc = jnp.broadcast_to(dwb_ref[...], (CONV_ROWS, CONV_CH))
        for jj in range(CONV_K):
            acc = acc + dww_ref[jj:jj + 1, :] * cbuf_ref[pl.ds(rc * CONV_ROWS + base + jj, CONV_ROWS), :]
        conv_ref[rc * CONV_ROWS:(rc + 1) * CONV_ROWS, :] = acc
    cbuf_ref[0:CONV_HIST, :] = cbuf_ref[ts:ts + CONV_HIST, :]
    cv = conv_ref[...]
    d = cv - jnp.mean(cv, axis=-1, keepdims=True)
    ln = d * lax.rsqrt(jnp.mean(d * d, axis=-1, keepdims=True) + LN_EPS) * lng_ref[...] + lnb_ref[...]
    yconv = (ln * jax.nn.sigmoid(ln)).astype(BF16)
    b_out = jnp.dot(yconv, wconv_ref[...], preferred_element_type=F32) + bconv_ref[...]

    p3 = jnp.dot(h, win_ref[:, c1:], preferred_element_type=F32)
    dm = x.shape[1]
    mixed = jax.nn.sigmoid(p3[:, 0:dm]) * a_out + jax.nn.sigmoid(p3[:, dm:2 * dm]) * b_out
    o_ref[...] = x + jnp.dot(mixed.astype(BF16), wout_ref[...], preferred_element_type=F32)


def _mixer_tables(seq, ts):
    half = RET_QK_DIM // 2
    inv = 1.0 / (ROPE_BASE ** (np.arange(half, dtype=np.float64) / half))
    ang = np.arange(seq, dtype=np.float64)[:, None] * inv[None, :]
    cos = np.tile(np.concatenate([np.cos(ang), np.cos(ang)], axis=1), (1, RET_HEADS))
    sin = np.tile(np.concatenate([-np.sin(ang), np.sin(ang)], axis=1), (1, RET_HEADS))
    log_gamma = np.log1p(-(2.0 ** (-5.0 - np.arange(RET_HEADS, dtype=np.float64))))
    n = np.arange(ts, dtype=np.float64)
    diff = n[:, None] - n[None, :]
    intra = np.where(diff[None] >= 0, np.exp(log_gamma[:, None, None] * np.maximum(diff, 0.0)[None]), 0.0)
    q_decay = np.exp(log_gamma[:, None] * (n + 1.0)[None])
    k_decay = np.exp(log_gamma[:, None] * (ts - 1.0 - n)[None])
    chunk_decay = np.exp(log_gamma * ts)
    qd = np.repeat(q_decay.T, RET_V_DIM, axis=1)
    kd = np.repeat(k_decay.T, RET_QK_DIM, axis=1)
    cd = np.repeat(chunk_decay, RET_V_DIM)[None, :]
    row_head = np.arange(RET_QK) // RET_QK_DIM
    col_head = np.arange(RET_WIDTH) // RET_V_DIM
    bd = (row_head[:, None] == col_head[None, :]).astype(np.float64)
    hm = (np.arange(RET_HEADS)[:, None] == row_head[None, :]).astype(np.float64)
    as32 = lambda a: jnp.asarray(a, dtype=F32)
    return tuple(as32(a) for a in (cos, sin, qd, kd, intra, cd, bd, hm))


def _mixer_tile(seq):
    for ts in (256, 128, 64):
        if seq % ts == 0:
            return ts
    raise ValueError(f"sequence length {seq} must be a multiple of {CONV_ROWS}")


def _mixer(x, g, w_in, gn_g, dw_w, dw_b, ln_g, ln_b, w_ret, w_conv, b_conv, w_out):
    bsz, seq, dm = x.shape
    ts = _mixer_tile(seq)
    cos, sin, qd, kd, intra, cd, bd, hm = _mixer_tables(seq, ts)
    row = lambda a: a.reshape(1, -1).astype(F32)
    in_cols = w_in.shape[1]
    grid = (bsz, seq // ts)
    in_specs = [
        pl.BlockSpec((None, ts, dm), lambda b, j: (b, j, 0)),
        _const_spec((1, dm)),
        _const_spec((dm, in_cols)),
        pl.BlockSpec((ts, RET_QK), lambda b, j: (j, 0)),
        pl.BlockSpec((ts, RET_QK), lambda b, j: (j, 0)),
        _const_spec((ts, RET_WIDTH)),
        _const_spec((ts, RET_QK)),
        _const_spec((RET_HEADS, ts, ts)),
        _const_spec((1, RET_WIDTH)),
        _const_spec((RET_QK, RET_WIDTH)),
        _const_spec((RET_HEADS, RET_QK)),
        _const_spec((1, RET_WIDTH)),
        _const_spec((CONV_K, CONV_CH)),
        _const_spec((1, CONV_CH)),
        _const_spec((1, CONV_CH)),
        _const_spec((1, CONV_CH)),
        _const_spec((RET_WIDTH, dm)),
        _const_spec((CONV_CH, dm)),
        _const_spec((1, dm)),
        _const_spec((dm, dm)),
    ]
    return pl.pallas_call(
        functools.partial(_mixer_kernel, ts=ts),
        name="mixer",
        grid=grid,
        in_specs=in_specs,
        out_specs=pl.BlockSpec((None, ts, dm), lambda b, j: (b, j, 0)),
        out_shape=jax.ShapeDtypeStruct((bsz, seq, dm), F32),
        scratch_shapes=[
            pltpu.VMEM((RET_QK, RET_WIDTH), F32),
            pltpu.VMEM((CONV_HIST + ts, CONV_CH), F32),
            pltpu.VMEM((ts, CONV_CH), F32),
        ],
        compiler_params=pltpu.CompilerParams(
            dimension_semantics=("arbitrary", "arbitrary"), vmem_limit_bytes=VMEM_LIMIT_BYTES),
    )(x, row(g), w_in.astype(BF16), cos, sin, qd, kd, intra, cd, bd, hm, row(gn_g), dw_w.astype(F32),
      row(dw_b), row(ln_g), row(ln_b), w_ret.astype(BF16), w_conv.astype(BF16), row(b_conv),
      w_out.astype(BF16))


def _top16(s, row_id):
    cur = s
    rank = jnp.full(s.shape, NOT_SELECTED, F32)
    vals = []
    for r in range(PEER_TOPK):
        m = jnp.max(cur, axis=0, keepdims=True)
        first = jnp.min(jnp.where(cur == m, row_id, 1e9), axis=0, keepdims=True)
        hit = row_id == first
        rank = jnp.where(hit, float(r), rank)
        cur = jnp.where(hit, -jnp.inf, cur)
        vals.append(m)
    return jnp.concatenate(vals, axis=0), rank


def _staircase(v1, v2, e1x, e2x):
    k = PEER_TOPK
    row = lax.broadcasted_iota(jnp.int32, (SUBLANES, LANES), 0).astype(F32)
    ninf = jnp.full((SUBLANES, LANES), -jnp.inf, F32)
    cand, pos, wgt = [], [], []
    for b0 in (0, SUBLANES):
        cand.append(v1[0:1] + v2[b0:b0 + SUBLANES]); pos.append(row + float(b0))
        wgt.append(e1x[0:1] * e2x[b0:b0 + SUBLANES])
    cand.append(v1[1:2] + v2[0:SUBLANES]); pos.append(row + float(k)); wgt.append(e1x[1:2] * e2x[0:SUBLANES])
    cand.append(v1[SUBLANES:k] + v2[0:1]); pos.append((row + float(SUBLANES)) * float(k))
    wgt.append(e1x[SUBLANES:k] * e2x[0:1])
    n_col = 5
    for b in range(n_col):
        a_max = float(k // (b + 1) - 1)
        ok = (row >= 2.0) & (row <= a_max)
        cand.append(jnp.where(ok, v1[0:SUBLANES] + v2[b:b + 1], ninf))
        pos.append(jnp.where(ok, row * float(k) + float(b), -1.0))
        wgt.append(e1x[0:SUBLANES] * e2x[b:b + 1])
    ng = len(cand)
    sel = [jnp.zeros((SUBLANES, LANES), F32) for _ in range(ng)]
    for _ in range(k):
        m = cand[0]
        for g in range(1, ng):
            m = jnp.maximum(m, cand[g])
        m = jnp.max(m, axis=0, keepdims=True)
        fp = jnp.where(cand[0] == m, pos[0], 1e9)
        for g in range(1, ng):
            fp = jnp.minimum(fp, jnp.where(cand[g] == m, pos[g], 1e9))
        fp = jnp.min(fp, axis=0, keepdims=True)
        for g in range(ng):
            hit = pos[g] == fp
            sel[g] = jnp.where(hit, 1.0, sel[g])
            cand[g] = jnp.where(hit, -jnp.inf, cand[g])
    b0cnt = jnp.sum(sel[0] + sel[1], axis=0, keepdims=True)
    b1cnt = jnp.sum(sel[2], axis=0, keepdims=True)
    low = sel[4]
    for g in range(5, ng):
        low = low + sel[g]
    low = low + jnp.where(row == 0.0, b0cnt, 0.0) + jnp.where(row == 1.0, b1cnt, 0.0)
    counts = jnp.concatenate([low, sel[3]], axis=0)
    zt = sel[0] * wgt[0]
    for g in range(1, ng):
        zt = zt + sel[g] * wgt[g]
    return counts, jnp.sum(zt, axis=0, keepdims=True)


def _peer_kernel(x_ref, g_ref, wq_ref, keys_ref, u0_ref, u_ref, vt_ref, gfin_ref, o_ref,
                 hn_ref, s_ref, rank2_ref, w2_ref, bcnt_ref, w1_ref, st_ref, at_ref, acc_ref,
                 *, tt, eb, final_norm):
    j = pl.program_id(1)
    n_blk = pl.num_programs(1) - 1
    nch = tt // PEER_CHUNK
    lpc = PEER_CHUNK // LANES
    qd = 2 * PEER_HALF
    nt_dims = (((1,), (1,)), ((), ()))

    @pl.when(j == 0)
    def _routing():
        x = x_ref[...]
        hn = (x * lax.rsqrt(jnp.mean(x * x, axis=-1, keepdims=True) + RMS_EPS) * g_ref[...]).astype(BF16)
        for c in range(nch):
            hn_c = hn[c * PEER_CHUNK:(c + 1) * PEER_CHUNK, :]
            hn_ref[c] = hn_c
            acc_ref[c] = jnp.zeros(acc_ref.shape[1:], F32)
            at_ref[1, c] = jnp.zeros(at_ref.shape[2:], BF16)
            st_ref[0, c] = lax.dot_general(u0_ref[...], hn_c, nt_dims, preferred_element_type=F32)
            for hh in range(PEER_HEADS):
                qt = lax.dot_general(wq_ref[hh * qd:(hh + 1) * qd, :], hn_c, nt_dims,
                                     preferred_element_type=F32).astype(BF16)
                for p in range(2):
                    sc = jnp.dot(keys_ref[2 * hh + p], qt[p * PEER_HALF:(p + 1) * PEER_HALF, :],
                                 preferred_element_type=F32)
                    for l in range(lpc):
                        s_ref[2 * hh + p, c * lpc + l] = sc[:, l * LANES:(l + 1) * LANES]

        row_id = lax.broadcasted_iota(jnp.int32, (N_KEYS, LANES), 0).astype(F32)

        def route(i, carry):
            hh = i // (nch * lpc)
            lb = i % (nch * lpc)
            s1 = s_ref[2 * hh, lb]
            s2 = s_ref[2 * hh + 1, lb]
            v1, rank1 = _top16(s1, row_id)
            v2, rank2 = _top16(s2, row_id)
            e1x = jnp.exp(v1 - v1[0:1])
            e2x = jnp.exp(v2 - v2[0:1])
            counts, z = _staircase(v1, v2, e1x, e2x)
            bcnt = jnp.zeros((N_KEYS, LANES), F32)
            for a in range(PEER_TOPK):
                bcnt = jnp.where(rank1 == float(a), counts[a:a + 1], bcnt)
            bcnt_ref[hh, lb] = bcnt
            w1_ref[hh, lb] = jnp.exp(s1 - v1[0:1]) * (0.5 / z)
            w2_ref[hh, lb] = jnp.exp(s2 - v2[0:1]).astype(BF16)
            rank2_ref[hh, lb] = rank2.astype(BF16)
            return carry

        lax.fori_loop(0, PEER_HEADS * nch * lpc, route, 0)

    def steady(cur, nxt):
        n_part = 2
        epp = eb // n_part
        dm = acc_ref.shape[1]

        def part(t, carry):
            c = t // n_part
            hf = t % n_part
            d0 = pl.multiple_of(hf * (dm // n_part), dm // n_part)
            r0 = pl.multiple_of(hf * (epp * N_KEYS), epp * N_KEYS)
            acc_ref[c, pl.ds(d0, dm // n_part), :] += jnp.dot(
                vt_ref[pl.ds(d0, dm // n_part), :], at_ref[nxt, c], preferred_element_type=F32)
            for e in range(epp):
                e1 = j * eb + hf * epp + e
                for l in range(lpc):
                    lb = c * lpc + l
                    cols = slice(l * LANES, (l + 1) * LANES)
                    cnt = [jnp.broadcast_to(bcnt_ref[hh, lb, pl.ds(e1, 1), :].astype(BF16), (BUILD_ROWS, LANES))
                           for hh in range(PEER_HEADS)]
                    w1 = [jnp.broadcast_to(w1_ref[hh, lb, pl.ds(e1, 1), :].astype(BF16), (BUILD_ROWS, LANES))
                          for hh in range(PEER_HEADS)]
                    for g in range(N_KEYS // BUILD_ROWS):
                        grow = slice(g * BUILD_ROWS, (g + 1) * BUILD_ROWS)
                        rows = pl.ds(r0 + e * N_KEYS + g * BUILD_ROWS, BUILD_ROWS)
                        xs = st_ref[cur, c, rows, cols]
                        ge = xs * (1.0 + lax.erf(xs * INV_SQRT2))
                        gate = jnp.zeros((BUILD_ROWS, LANES), BF16)
                        for hh in range(PEER_HEADS):
                            w2 = w2_ref[hh, lb, grow, :]
                            gate = gate + jnp.where(rank2_ref[hh, lb, grow, :] < cnt[hh], w2,
                                                    jnp.zeros_like(w2)) * w1[hh]
                        at_ref[cur, c, rows, cols] = ge.astype(BF16) * gate
            st_ref[nxt, c, pl.ds(r0, epp * N_KEYS), :] = lax.dot_general(
                u_ref[pl.ds(r0, epp * N_KEYS), :], hn_ref[c], nt_dims, preferred_element_type=F32)
            return carry

        lax.fori_loop(0, nch * n_part, part, 0)

    @pl.when((j < n_blk) & (j % 2 == 0))
    def _even():
        steady(0, 1)

    @pl.when((j < n_blk) & (j % 2 == 1))
    def _odd():
        steady(1, 0)

    @pl.when(j == n_blk)
    def _finish():
        last = (N_KEYS // eb - 1) % 2
        for c in range(nch):
            acc = acc_ref[c] + jnp.dot(vt_ref[...], at_ref[last, c], preferred_element_type=F32)
            rows = slice(c * PEER_CHUNK, (c + 1) * PEER_CHUNK)
            y = acc.T + x_ref[rows, :]
            if final_norm:
                y = y * lax.rsqrt(jnp.mean(y * y, axis=-1, keepdims=True) + RMS_EPS) * gfin_ref[...]
            o_ref[rows, :] = y


def _peer_tiles(tokens):
    for tt in (512, 256):
        if tokens % tt == 0:
            return tt, 8
    raise ValueError(f"token count {tokens} must be a multiple of {PEER_CHUNK}")


def _peer(x, g, w_query, sub_keys, u_tab, v_tab, g_final, final_norm):
    tokens, dm = x.shape
    tt, eb = _peer_tiles(tokens)
    nlb = tt // LANES
    nch = tt // PEER_CHUNK
    eblk = eb * N_KEYS
    n_exp = u_tab.shape[0]
    assert n_exp == N_KEYS * N_KEYS and N_KEYS % eb == 0
    assert sub_keys.shape == (PEER_HEADS, 2, N_KEYS, PEER_HALF)
    n_blk = N_KEYS // eb
    wq_t = w_query.T.astype(BF16)
    keys = sub_keys.reshape(PEER_HEADS * 2, N_KEYS, PEER_HALF).astype(BF16)
    u_bf = u_tab.astype(BF16)
    vt_bf = v_tab.T.astype(BF16)
    row = lambda a: a.reshape(1, -1).astype(F32)
    grid = (tokens // tt, n_blk + 1)
    return pl.pallas_call(
        functools.partial(_peer_kernel, tt=tt, eb=eb, final_norm=final_norm),
        name="peer",
        grid=grid,
        in_specs=[
            pl.BlockSpec((tt, dm), lambda i, j: (i, 0)),
            _const_spec((1, dm)),
            _const_spec(wq_t.shape),
            _const_spec(keys.shape),
            _const_spec((eblk, dm)),
            pl.BlockSpec((eblk, dm), lambda i, j: (jnp.minimum(j + 1, n_blk - 1), 0)),
            pl.BlockSpec((dm, eblk), lambda i, j: (0, jnp.clip(j - 1, 0, n_blk - 1))),
            _const_spec((1, dm)),
        ],
        out_specs=pl.BlockSpec((tt, dm), lambda i, j: (i, 0)),
        out_shape=jax.ShapeDtypeStruct((tokens, dm), F32),
        scratch_shapes=[
            pltpu.VMEM((nch, PEER_CHUNK, dm), BF16),
            pltpu.VMEM((2 * PEER_HEADS, nlb, N_KEYS, LANES), F32),
            pltpu.VMEM((PEER_HEADS, nlb, N_KEYS, LANES), BF16),
            pltpu.VMEM((PEER_HEADS, nlb, N_KEYS, LANES), BF16),
            pltpu.VMEM((PEER_HEADS, nlb, N_KEYS, LANES), F32),
            pltpu.VMEM((PEER_HEADS, nlb, N_KEYS, LANES), F32),
            pltpu.VMEM((2, nch, eblk, PEER_CHUNK), F32),
            pltpu.VMEM((2, nch, eblk, PEER_CHUNK), BF16),
            pltpu.VMEM((nch, dm, PEER_CHUNK), F32),
        ],
        compiler_params=pltpu.CompilerParams(
            dimension_semantics=("arbitrary", "arbitrary"), vmem_limit_bytes=VMEM_LIMIT_BYTES),
    )(x, row(g), wq_t, keys, u_bf, u_bf, vt_bf, row(g_final))


def kernel(x, rms_mix_g, w_in, ret_gn_g, conv_dw_w, conv_dw_b, conv_ln_g, conv_ln_b, w_ret_proj, w_conv_proj,
           b_conv_proj, w_out, rms_ffn_g, w_query, peer_sub_keys, peer_u, peer_v, rms_final_g):
    bsz, seq, dm = x.shape
    depth = w_in.shape[0]
    for l in range(depth):
        x = _mixer(x, rms_mix_g[l], w_in[l], ret_gn_g[l], conv_dw_w[l], conv_dw_b[l], conv_ln_g[l],
                   conv_ln_b[l], w_ret_proj[l], w_conv_proj[l], b_conv_proj[l], w_out[l])
        x = _peer(x.reshape(bsz * seq, dm), rms_ffn_g[l], w_query[l], peer_sub_keys[l], peer_u[l], peer_v[l],
                  rms_final_g, final_norm=(l == depth - 1)).reshape(bsz, seq, dm)
    return x
```

```python
import functools

import numpy as np
import jax
import jax.numpy as jnp
from jax import lax
from jax.experimental import pallas as pl
from jax.experimental.pallas import tpu as pltpu

F32 = jnp.float32
BF16 = jnp.bfloat16

RET_HEADS = 4
RET_QK_DIM = 64
RET_V_DIM = 128
RET_QK = RET_HEADS * RET_QK_DIM
RET_WIDTH = RET_HEADS * RET_V_DIM
ROPE_BASE = 10000.0
CONV_CH = 512
CONV_K = 31
PEER_HEADS = 8
PEER_HALF = 128
N_KEYS = 128
PEER_TOPK = 16
RMS_EPS = 1e-6
LN_EPS = 1e-5
INV_SQRT2 = 0.7071067811865476

LANES = 128
SUBLANES = 8
VMEM_LIMIT_BYTES = 56 * 1024 * 1024
CONV_HIST = 32
CONV_ROWS = 64
PEER_CHUNK = 256
BUILD_ROWS = 32
NOT_SELECTED = 127.0


def _const_spec(shape):
    nd = len(shape)
    return pl.BlockSpec(shape, lambda *_: (0,) * nd, pipeline_mode=pl.Buffered(1))


def _swap_halves(t):
    half = RET_QK_DIM // 2
    parts = []
    for c in range(t.shape[1] // LANES):
        tc = t[:, c * LANES:(c + 1) * LANES]
        lane = lax.broadcasted_iota(jnp.int32, tc.shape, 1)
        first = (lane % RET_QK_DIM) < half
        nxt = pltpu.roll(tc, LANES - half, 1)
        prv = pltpu.roll(tc, half, 1)
        parts.append(jnp.where(first, nxt, prv))
    return jnp.concatenate(parts, axis=1)


def _mixer_kernel(x_ref, g_ref, win_ref, cos_ref, sin_ref, qd_ref, kd_ref, intra_ref, cd_ref, bd_ref,
                  hm_ref, gn_ref, dww_ref, dwb_ref, lng_ref, lnb_ref, wret_ref, wconv_ref, bconv_ref,
                  wout_ref, o_ref, state_ref, cbuf_ref, conv_ref, *, ts):
    @pl.when(pl.program_id(1) == 0)
    def _():
        state_ref[...] = jnp.zeros_like(state_ref)
        cbuf_ref[0:CONV_HIST, :] = jnp.zeros((CONV_HIST, CONV_CH), F32)

    x = x_ref[...]
    h = (x * lax.rsqrt(jnp.mean(x * x, axis=-1, keepdims=True) + RMS_EPS) * g_ref[...]).astype(BF16)

    c0 = 2 * RET_QK + RET_WIDTH
    qkv = jnp.dot(h, win_ref[:, 0:c0], preferred_element_type=F32)
    q = qkv[:, 0:RET_QK]
    k = qkv[:, RET_QK:2 * RET_QK]
    vb = qkv[:, 2 * RET_QK:c0].astype(BF16)
    cos = cos_ref[...]
    sin = sin_ref[...]
    qr = q * cos + _swap_halves(q) * sin
    kr = (k * cos + _swap_halves(k) * sin) * (RET_QK_DIM ** -0.5)
    qb = qr.astype(BF16)
    kb = kr.astype(BF16)
    inner = []
    for hh in range(RET_HEADS):
        qm = (qr * hm_ref[hh:hh + 1, :]).astype(BF16)
        sc = lax.dot_general(qm, kb, (((1,), (1,)), ((), ())), preferred_element_type=F32)
        sc = sc * intra_ref[hh]
        inner.append(jnp.dot(sc.astype(BF16), vb[:, hh * RET_V_DIM:(hh + 1) * RET_V_DIM],
                             preferred_element_type=F32))
    st = state_ref[...]
    cross = jnp.dot(qb, st.astype(BF16), preferred_element_type=F32) * qd_ref[...]
    o = jnp.concatenate(inner, axis=1) + cross
    kv = lax.dot_general((kr * kd_ref[...]).astype(BF16), vb, (((0,), (0,)), ((), ())),
                         preferred_element_type=F32)
    state_ref[...] = st * cd_ref[...] + kv * bd_ref[...]

    normed = []
    for hh in range(RET_HEADS):
        oh = o[:, hh * RET_V_DIM:(hh + 1) * RET_V_DIM]
        d = oh - jnp.mean(oh, axis=-1, keepdims=True)
        normed.append(d * lax.rsqrt(jnp.mean(d * d, axis=-1, keepdims=True) + LN_EPS))
    on = jnp.concatenate(normed, axis=1) * gn_ref[...]

    c1 = c0 + RET_WIDTH + 2 * CONV_CH
    p2 = jnp.dot(h, win_ref[:, c0:c1], preferred_element_type=F32)
    g_ret = p2[:, 0:RET_WIDTH]
    yret = (g_ret * jax.nn.sigmoid(g_ret) * on).astype(BF16)
    a_out = jnp.dot(yret, wret_ref[...], preferred_element_type=F32)

    u_val = p2[:, RET_WIDTH:RET_WIDTH + CONV_CH]
    u_gate = p2[:, RET_WIDTH + CONV_CH:RET_WIDTH + 2 * CONV_CH]
    cbuf_ref[CONV_HIST:CONV_HIST + ts, :] = u_val * jax.nn.sigmoid(u_gate)
    base = CONV_HIST - (CONV_K - 1)
    for rc in range(ts // CONV_ROWS):
        acc = jnp.broadcast_to(dwb_ref[...], (CONV_ROWS, CONV_CH))
        for jj in range(CONV_K):
            acc = acc + dww_ref[jj:jj + 1, :] * cbuf_ref[pl.ds(rc * CONV_ROWS + base + jj, CONV_ROWS), :]
        conv_ref[rc * CONV_ROWS:(rc + 1) * CONV_ROWS, :] = acc
    cbuf_ref[0:CONV_HIST, :] = cbuf_ref[ts:ts + CONV_HIST, :]
    cv = conv_ref[...]
    d = cv - jnp.mean(cv, axis=-1, keepdims=True)
    ln = d * lax.rsqrt(jnp.mean(d * d, axis=-1, keepdims=True) + LN_EPS) * lng_ref[...] + lnb_ref[...]
    yconv = (ln * jax.nn.sigmoid(ln)).astype(BF16)
    b_out = jnp.dot(yconv, wconv_ref[...], preferred_element_type=F32) + bconv_ref[...]

    p3 = jnp.dot(h, win_ref[:, c1:], preferred_element_type=F32)
    dm = x.shape[1]
    mixed = jax.nn.sigmoid(p3[:, 0:dm]) * a_out + jax.nn.sigmoid(p3[:, dm:2 * dm]) * b_out
    o_ref[...] = x + jnp.dot(mixed.astype(BF16), wout_ref[...], preferred_element_type=F32)


def _mixer_tables(seq, ts):
    half = RET_QK_DIM // 2
    inv = 1.0 / (ROPE_BASE ** (np.arange(half, dtype=np.float64) / half))
    ang = np.arange(seq, dtype=np.float64)[:, None] * inv[None, :]
    cos = np.tile(np.concatenate([np.cos(ang), np.cos(ang)], axis=1), (1, RET_HEADS))
    sin = np.tile(np.concatenate([-np.sin(ang), np.sin(ang)], axis=1), (1, RET_HEADS))
    log_gamma = np.log1p(-(2.0 ** (-5.0 - np.arange(RET_HEADS, dtype=np.float64))))
    n = np.arange(ts, dtype=np.float64)
    diff = n[:, None] - n[None, :]
    intra = np.where(diff[None] >= 0, np.exp(log_gamma[:, None, None] * np.maximum(diff, 0.0)[None]), 0.0)
    q_decay = np.exp(log_gamma[:, None] * (n + 1.0)[None])
    k_decay = np.exp(log_gamma[:, None] * (ts - 1.0 - n)[None])
    chunk_decay = np.exp(log_gamma * ts)
    qd = np.repeat(q_decay.T, RET_V_DIM, axis=1)
    kd = np.repeat(k_decay.T, RET_QK_DIM, axis=1)
    cd = np.repeat(chunk_decay, RET_V_DIM)[None, :]
    row_head = np.arange(RET_QK) // RET_QK_DIM
    col_head = np.arange(RET_WIDTH) // RET_V_DIM
    bd = (row_head[:, None] == col_head[None, :]).astype(np.float64)
    hm = (np.arange(RET_HEADS)[:, None] == row_head[None, :]).astype(np.float64)
    as32 = lambda a: jnp.asarray(a, dtype=F32)
    return tuple(as32(a) for a in (cos, sin, qd, kd, intra, cd, bd, hm))


def _mixer_tile(seq):
    for ts in (256, 128, 64):
        if seq % ts == 0:
            return ts
    raise ValueError(f"sequence length {seq} must be a multiple of {CONV_ROWS}")


def _mixer(x, g, w_in, gn_g, dw_w, dw_b, ln_g, ln_b, w_ret, w_conv, b_conv, w_out):
    bsz, seq, dm = x.shape
    ts = _mixer_tile(seq)
    cos, sin, qd, kd, intra, cd, bd, hm = _mixer_tables(seq, ts)
    row = lambda a: a.reshape(1, -1).astype(F32)
    in_cols = w_in.shape[1]
    grid = (bsz, seq // ts)
    in_specs = [
        pl.BlockSpec((None, ts, dm), lambda b, j: (b, j, 0)),
        _const_spec((1, dm)),
        _const_spec((dm, in_cols)),
        pl.BlockSpec((ts, RET_QK), lambda b, j: (j, 0)),
        pl.BlockSpec((ts, RET_QK), lambda b, j: (j, 0)),
        _const_spec((ts, RET_WIDTH)),
        _const_spec((ts, RET_QK)),
        _const_spec((RET_HEADS, ts, ts)),
        _const_spec((1, RET_WIDTH)),
        _const_spec((RET_QK, RET_WIDTH)),
        _const_spec((RET_HEADS, RET_QK)),
        _const_spec((1, RET_WIDTH)),
        _const_spec((CONV_K, CONV_CH)),
        _const_spec((1, CONV_CH)),
        _const_spec((1, CONV_CH)),
        _const_spec((1, CONV_CH)),
        _const_spec((RET_WIDTH, dm)),
        _const_spec((CONV_CH, dm)),
        _const_spec((1, dm)),
        _const_spec((dm, dm)),
    ]
    return pl.pallas_call(
        functools.partial(_mixer_kernel, ts=ts),
        name="mixer",
        grid=grid,
        in_specs=in_specs,
        out_specs=pl.BlockSpec((None, ts, dm), lambda b, j: (b, j, 0)),
        out_shape=jax.ShapeDtypeStruct((bsz, seq, dm), F32),
        scratch_shapes=[
            pltpu.VMEM((RET_QK, RET_WIDTH), F32),
            pltpu.VMEM((CONV_HIST + ts, CONV_CH), F32),
            pltpu.VMEM((ts, CONV_CH), F32),
        ],
        compiler_params=pltpu.CompilerParams(
            dimension_semantics=("arbitrary", "arbitrary"), vmem_limit_bytes=VMEM_LIMIT_BYTES),
    )(x, row(g), w_in.astype(BF16), cos, sin, qd, kd, intra, cd, bd, hm, row(gn_g), dw_w.astype(F32),
      row(dw_b), row(ln_g), row(ln_b), w_ret.astype(BF16), w_conv.astype(BF16), row(b_conv),
      w_out.astype(BF16))


def _top16(s, row_id, break_ties):
    cur = s
    rank = jnp.full(s.shape, NOT_SELECTED, F32)
    vals = []
    for r in range(PEER_TOPK):
        m = jnp.max(cur, axis=0, keepdims=True)
        if break_ties:
            first = jnp.min(jnp.where(cur == m, row_id, 1e9), axis=0, keepdims=True)
            hit = row_id == first
        else:
            hit = cur == m
        rank = jnp.where(hit, float(r), rank)
        cur = jnp.where(hit, -jnp.inf, cur)
        vals.append(m)
    return jnp.concatenate(vals, axis=0), rank


def _staircase(v1, v2, e1x, e2x, break_ties):
    k = PEER_TOPK
    row = lax.broadcasted_iota(jnp.int32, (SUBLANES, LANES), 0).astype(F32)
    ninf = jnp.full((SUBLANES, LANES), -jnp.inf, F32)
    cand, pos, wgt = [], [], []
    for b0 in (0, SUBLANES):
        cand.append(v1[0:1] + v2[b0:b0 + SUBLANES]); pos.append(row + float(b0))
        wgt.append(e1x[0:1] * e2x[b0:b0 + SUBLANES])
    cand.append(v1[1:2] + v2[0:SUBLANES]); pos.append(row + float(k)); wgt.append(e1x[1:2] * e2x[0:SUBLANES])
    cand.append(v1[SUBLANES:k] + v2[0:1]); pos.append((row + float(SUBLANES)) * float(k))
    wgt.append(e1x[SUBLANES:k] * e2x[0:1])
    n_col = 5
    for b in range(n_col):
        a_max = float(k // (b + 1) - 1)
        ok = (row >= 2.0) & (row <= a_max)
        cand.append(jnp.where(ok, v1[0:SUBLANES] + v2[b:b + 1], ninf))
        pos.append(jnp.where(ok, row * float(k) + float(b), -1.0))
        wgt.append(e1x[0:SUBLANES] * e2x[b:b + 1])
    ng = len(cand)
    sel = [jnp.zeros((SUBLANES, LANES), F32) for _ in range(ng)]
    for _ in range(k):
        m = cand[0]
        for g in range(1, ng):
            m = jnp.maximum(m, cand[g])
        m = jnp.max(m, axis=0, keepdims=True)
        if break_ties:
            fp = jnp.where(cand[0] == m, pos[0], 1e9)
            for g in range(1, ng):
                fp = jnp.minimum(fp, jnp.where(cand[g] == m, pos[g], 1e9))
            fp = jnp.min(fp, axis=0, keepdims=True)
        for g in range(ng):
            hit = (pos[g] == fp) if break_ties else (cand[g] == m)
            sel[g] = jnp.where(hit, 1.0, sel[g])
            cand[g] = jnp.where(hit, -jnp.inf, cand[g])
    b0cnt = jnp.sum(sel[0] + sel[1], axis=0, keepdims=True)
    b1cnt = jnp.sum(sel[2], axis=0, keepdims=True)
    low = sel[4]
    for g in range(5, ng):
        low = low + sel[g]
    low = low + jnp.where(row == 0.0, b0cnt, 0.0) + jnp.where(row == 1.0, b1cnt, 0.0)
    counts = jnp.concatenate([low, sel[3]], axis=0)
    zt = sel[0] * wgt[0]
    nsel = sel[0]
    for g in range(1, ng):
        zt = zt + sel[g] * wgt[g]
        nsel = nsel + sel[g]
    return counts, jnp.sum(zt, axis=0, keepdims=True), jnp.sum(nsel, axis=0, keepdims=True)


def _peer_kernel(x_ref, g_ref, wq_ref, keys_ref, u0_ref, u_ref, vt_ref, gfin_ref, o_ref,
                 hn_ref, s_ref, rank2_ref, w2_ref, bcnt_ref, w1_ref, st_ref, at_ref, acc_ref,
                 *, tt, eb, final_norm):
    j = pl.program_id(1)
    n_blk = pl.num_programs(1) - 1
    nch = tt // PEER_CHUNK
    lpc = PEER_CHUNK // LANES
    qd = 2 * PEER_HALF
    nt_dims = (((1,), (1,)), ((), ()))

    @pl.when(j == 0)
    def _routing():
        x = x_ref[...]
        hn = (x * lax.rsqrt(jnp.mean(x * x, axis=-1, keepdims=True) + RMS_EPS) * g_ref[...]).astype(BF16)
        for c in range(nch):
            hn_c = hn[c * PEER_CHUNK:(c + 1) * PEER_CHUNK, :]
            hn_ref[c] = hn_c
            acc_ref[c] = jnp.zeros(acc_ref.shape[1:], F32)
            at_ref[1, c] = jnp.zeros(at_ref.shape[2:], BF16)
            st_ref[0, c] = lax.dot_general(u0_ref[...], hn_c, nt_dims, preferred_element_type=F32)
            for hh in range(PEER_HEADS):
                qt = lax.dot_general(wq_ref[hh * qd:(hh + 1) * qd, :], hn_c, nt_dims,
                                     preferred_element_type=F32).astype(BF16)
                for p in range(2):
                    sc = jnp.dot(keys_ref[2 * hh + p], qt[p * PEER_HALF:(p + 1) * PEER_HALF, :],
                                 preferred_element_type=F32)
                    for l in range(lpc):
                        s_ref[2 * hh + p, c * lpc + l] = sc[:, l * LANES:(l + 1) * LANES]

        row_id = lax.broadcasted_iota(jnp.int32, (N_KEYS, LANES), 0).astype(F32)

        def route(i, carry):
            hh = i // (nch * lpc)
            lb = i % (nch * lpc)
            s1 = s_ref[2 * hh, lb]
            s2 = s_ref[2 * hh + 1, lb]

            def solve(break_ties):
                v1, rank1 = _top16(s1, row_id, break_ties)
                v2, rank2 = _top16(s2, row_id, break_ties)
                e1x = jnp.exp(v1 - v1[0:1])
                e2x = jnp.exp(v2 - v2[0:1])
                counts, z, nsel = _staircase(v1, v2, e1x, e2x, break_ties)
                bcnt = jnp.zeros((N_KEYS, LANES), F32)
                for a in range(PEER_TOPK):
                    bcnt = jnp.where(rank1 == float(a), counts[a:a + 1], bcnt)
                bcnt_ref[hh, lb] = bcnt
                w1_ref[hh, lb] = jnp.exp(s1 - v1[0:1]) * (0.5 / z)
                w2_ref[hh, lb] = jnp.exp(s2 - v2[0:1]).astype(BF16)
                rank2_ref[hh, lb] = rank2.astype(BF16)
                ranked = jnp.where(rank1 < float(PEER_TOPK), 1.0, 0.0) + jnp.where(rank2 < float(PEER_TOPK), 1.0, 0.0)
                return jnp.sum(ranked, axis=0, keepdims=True) + nsel

            picked = solve(break_ties=False)
            tied = jnp.max(jnp.where(picked == float(3 * PEER_TOPK), 0.0, 1.0))

            @pl.when(tied > 0.0)
            def _():
                solve(break_ties=True)

            return carry

        lax.fori_loop(0, PEER_HEADS * nch * lpc, route, 0)

    def steady(cur, nxt):
        n_part = 2
        epp = eb // n_part
        dm = acc_ref.shape[1]

        def part(t, carry):
            c = t // n_part
            hf = t % n_part
            d0 = pl.multiple_of(hf * (dm // n_part), dm // n_part)
            r0 = pl.multiple_of(hf * (epp * N_KEYS), epp * N_KEYS)
            acc_ref[c, pl.ds(d0, dm // n_part), :] += jnp.dot(
                vt_ref[pl.ds(d0, dm // n_part), :], at_ref[nxt, c], preferred_element_type=F32)
            for e in range(epp):
                e1 = j * eb + hf * epp + e
                for l in range(lpc):
                    lb = c * lpc + l
                    cols = slice(l * LANES, (l + 1) * LANES)
                    cnt = [jnp.broadcast_to(bcnt_ref[hh, lb, pl.ds(e1, 1), :].astype(BF16), (BUILD_ROWS, LANES))
                           for hh in range(PEER_HEADS)]
                    w1 = [jnp.broadcast_to(w1_ref[hh, lb, pl.ds(e1, 1), :].astype(BF16), (BUILD_ROWS, LANES))
                          for hh in range(PEER_HEADS)]
                    for g in range(N_KEYS // BUILD_ROWS):
                        grow = slice(g * BUILD_ROWS, (g + 1) * BUILD_ROWS)
                        rows = pl.ds(r0 + e * N_KEYS + g * BUILD_ROWS, BUILD_ROWS)
                        xs = st_ref[cur, c, rows, cols]
                        ge = xs * (1.0 + lax.erf(xs * INV_SQRT2))
                        gate = jnp.zeros((BUILD_ROWS, LANES), BF16)
                        for hh in range(PEER_HEADS):
                            w2 = w2_ref[hh, lb, grow, :]
                            gate = gate + jnp.where(rank2_ref[hh, lb, grow, :] < cnt[hh], w2,
                                                    jnp.zeros_like(w2)) * w1[hh]
                        at_ref[cur, c, rows, cols] = ge.astype(BF16) * gate
            st_ref[nxt, c, pl.ds(r0, epp * N_KEYS), :] = lax.dot_general(
                u_ref[pl.ds(r0, epp * N_KEYS), :], hn_ref[c], nt_dims, preferred_element_type=F32)
            return carry

        lax.fori_loop(0, nch * n_part, part, 0)

    @pl.when((j < n_blk) & (j % 2 == 0))
    def _even():
        steady(0, 1)

    @pl.when((j < n_blk) & (j % 2 == 1))
    def _odd():
        steady(1, 0)

    @pl.when(j == n_blk)
    def _finish():
        last = (N_KEYS // eb - 1) % 2
        for c in range(nch):
            acc = acc_ref[c] + jnp.dot(vt_ref[...], at_ref[last, c], preferred_element_type=F32)
            rows = slice(c * PEER_CHUNK, (c + 1) * PEER_CHUNK)
            y = acc.T + x_ref[rows, :]
            if final_norm:
                y = y * lax.rsqrt(jnp.mean(y * y, axis=-1, keepdims=True) + RMS_EPS) * gfin_ref[...]
            o_ref[rows, :] = y


def _peer_tiles(tokens):
    for tt in (512, 256):
        if tokens % tt == 0:
            return tt, 8
    raise ValueError(f"token count {tokens} must be a multiple of {PEER_CHUNK}")


def _peer(x, g, w_query, sub_keys, u_tab, v_tab, g_final, final_norm):
    tokens, dm = x.shape
    tt, eb = _peer_tiles(tokens)
    nlb = tt // LANES
    nch = tt // PEER_CHUNK
    eblk = eb * N_KEYS
    n_exp = u_tab.shape[0]
    assert n_exp == N_KEYS * N_KEYS and N_KEYS % eb == 0
    assert sub_keys.shape == (PEER_HEADS, 2, N_KEYS, PEER_HALF)
    n_blk = N_KEYS // eb
    wq_t = w_query.T.astype(BF16)
    keys = sub_keys.reshape(PEER_HEADS * 2, N_KEYS, PEER_HALF).astype(BF16)
    u_bf = u_tab.astype(BF16)
    vt_bf = v_tab.T.astype(BF16)
    row = lambda a: a.reshape(1, -1).astype(F32)
    grid = (tokens // tt, n_blk + 1)
    return pl.pallas_call(
        functools.partial(_peer_kernel, tt=tt, eb=eb, final_norm=final_norm),
        name="peer",
        grid=grid,
        in_specs=[
            pl.BlockSpec((tt, dm), lambda i, j: (i, 0)),
            _const_spec((1, dm)),
            _const_spec(wq_t.shape),
            _const_spec(keys.shape),
            _const_spec((eblk, dm)),
            pl.BlockSpec((eblk, dm), lambda i, j: (jnp.minimum(j + 1, n_blk - 1), 0)),
            pl.BlockSpec((dm, eblk), lambda i, j: (0, jnp.clip(j - 1, 0, n_blk - 1))),
            _const_spec((1, dm)),
        ],
        out_specs=pl.BlockSpec((tt, dm), lambda i, j: (i, 0)),
        out_shape=jax.ShapeDtypeStruct((tokens, dm), F32),
        scratch_shapes=[
            pltpu.VMEM((nch, PEER_CHUNK, dm), BF16),
            pltpu.VMEM((2 * PEER_HEADS, nlb, N_KEYS, LANES), F32),
            pltpu.VMEM((PEER_HEADS, nlb, N_KEYS, LANES), BF16),
            pltpu.VMEM((PEER_HEADS, nlb, N_KEYS, LANES), BF16),
            pltpu.VMEM((PEER_HEADS, nlb, N_KEYS, LANES), F32),
            pltpu.VMEM((PEER_HEADS, nlb, N_KEYS, LANES), F32),
            pltpu.VMEM((2, nch, eblk, PEER_CHUNK), F32),
            pltpu.VMEM((2, nch, eblk, PEER_CHUNK), BF16),
            pltpu.VMEM((nch, dm, PEER_CHUNK), F32),
        ],
        compiler_params=pltpu.CompilerParams(
            dimension_semantics=("arbitrary", "arbitrary"), vmem_limit_bytes=VMEM_LIMIT_BYTES),
    )(x, row(g), wq_t, keys, u_bf, u_bf, vt_bf, row(g_final))


def kernel(x, rms_mix_g, w_in, ret_gn_g, conv_dw_w, conv_dw_b, conv_ln_g, conv_ln_b, w_ret_proj, w_conv_proj,
           b_conv_proj, w_out, rms_ffn_g, w_query, peer_sub_keys, peer_u, peer_v, rms_final_g):
    bsz, seq, dm = x.shape
    depth = w_in.shape[0]
    for l in range(depth):
        x = _mixer(x, rms_mix_g[l], w_in[l], ret_gn_g[l], conv_dw_w[l], conv_dw_b[l], conv_ln_g[l],
                   conv_ln_b[l], w_ret_proj[l], w_conv_proj[l], b_conv_proj[l], w_out[l])
        x = _peer(x.reshape(bsz * seq, dm), rms_ffn_g[l], w_query[l], peer_sub_keys[l], peer_u[l], peer_v[l],
                  rms_final_g, final_norm=(l == depth - 1)).reshape(bsz, seq, dm)
    return x
```

```python
import functools

import numpy as np
import jax
import jax.numpy as jnp
from jax import lax
from jax.experimental import pallas as pl
from jax.experimental.pallas import tpu as pltpu

F32 = jnp.float32
BF16 = jnp.bfloat16

RET_HEADS = 4
RET_QK_DIM = 64
RET_V_DIM = 128
RET_QK = RET_HEADS * RET_QK_DIM
RET_WIDTH = RET_HEADS * RET_V_DIM
ROPE_BASE = 10000.0
CONV_CH = 512
CONV_K = 31
PEER_HEADS = 8
PEER_HALF = 128
N_KEYS = 128
PEER_TOPK = 16
RMS_EPS = 1e-6
LN_EPS = 1e-5
INV_SQRT2 = 0.7071067811865476

LANES = 128
SUBLANES = 8
VMEM_LIMIT_BYTES = 56 * 1024 * 1024
CONV_HIST = 32
CONV_ROWS = 32
PEER_CHUNK = 256
BUILD_ROWS = 32
NOT_SELECTED = 127.0


def _const_spec(shape):
    nd = len(shape)
    return pl.BlockSpec(shape, lambda *_: (0,) * nd, pipeline_mode=pl.Buffered(1))


def _swap_halves(t):
    half = RET_QK_DIM // 2
    parts = []
    for c in range(t.shape[1] // LANES):
        tc = t[:, c * LANES:(c + 1) * LANES]
        lane = lax.broadcasted_iota(jnp.int32, tc.shape, 1)
        first = (lane % RET_QK_DIM) < half
        nxt = pltpu.roll(tc, LANES - half, 1)
        prv = pltpu.roll(tc, half, 1)
        parts.append(jnp.where(first, nxt, prv))
    return jnp.concatenate(parts, axis=1)


def _mixer_kernel(x_ref, g_ref, win_ref, cos_ref, sin_ref, qd_ref, kd_ref, intra_ref, cd_ref, bd_ref,
                  hm_ref, gn_ref, dww_ref, dwb_ref, lng_ref, lnb_ref, wret_ref, wconv_ref, bconv_ref,
                  wout_ref, o_ref, state_ref, cbuf_ref, conv_ref, shift_ref, *, ts):
    @pl.when(pl.program_id(1) == 0)
    def _():
        state_ref[...] = jnp.zeros_like(state_ref)
        cbuf_ref[0:CONV_HIST, :] = jnp.zeros((CONV_HIST, CONV_CH), F32)

    x = x_ref[...]
    h = (x * lax.rsqrt(jnp.mean(x * x, axis=-1, keepdims=True) + RMS_EPS) * g_ref[...]).astype(BF16)

    c0 = 2 * RET_QK + RET_WIDTH
    qkv = jnp.dot(h, win_ref[:, 0:c0], preferred_element_type=F32)
    q = qkv[:, 0:RET_QK]
    k = qkv[:, RET_QK:2 * RET_QK]
    vb = qkv[:, 2 * RET_QK:c0].astype(BF16)
    cos = cos_ref[...]
    sin = sin_ref[...]
    qr = q * cos + _swap_halves(q) * sin
    kr = (k * cos + _swap_halves(k) * sin) * (RET_QK_DIM ** -0.5)
    qb = qr.astype(BF16)
    kbt = kr.T.astype(BF16)
    inner = []
    for hh in range(RET_HEADS):
        qm = (qr * hm_ref[hh:hh + 1, :]).astype(BF16)
        sc = jnp.dot(qm, kbt, preferred_element_type=F32)
        sc = sc * intra_ref[hh]
        inner.append(jnp.dot(sc.astype(BF16), vb[:, hh * RET_V_DIM:(hh + 1) * RET_V_DIM],
                             preferred_element_type=F32))
    st = state_ref[...]
    cross = jnp.dot(qb, st.astype(BF16), preferred_element_type=F32) * qd_ref[...]
    o = jnp.concatenate(inner, axis=1) + cross
    kv = lax.dot_general((kr * kd_ref[...]).astype(BF16), vb, (((0,), (0,)), ((), ())),
                         preferred_element_type=F32)
    state_ref[...] = st * cd_ref[...] + kv * bd_ref[...]

    normed = []
    for hh in range(RET_HEADS):
        oh = o[:, hh * RET_V_DIM:(hh + 1) * RET_V_DIM]
        d = oh - jnp.mean(oh, axis=-1, keepdims=True)
        normed.append(d * lax.rsqrt(jnp.mean(d * d, axis=-1, keepdims=True) + LN_EPS))
    on = jnp.concatenate(normed, axis=1) * gn_ref[...]

    c1 = c0 + RET_WIDTH + 2 * CONV_CH
    p2 = jnp.dot(h, win_ref[:, c0:c1], preferred_element_type=F32)
    g_ret = p2[:, 0:RET_WIDTH]
    yret = (g_ret * jax.nn.sigmoid(g_ret) * on).astype(BF16)
    a_out = jnp.dot(yret, wret_ref[...], preferred_element_type=F32)

    u_val = p2[:, RET_WIDTH:RET_WIDTH + CONV_CH]
    u_gate = p2[:, RET_WIDTH + CONV_CH:RET_WIDTH + 2 * CONV_CH]
    cbuf_ref[CONV_HIST:CONV_HIST + ts, :] = u_val * jax.nn.sigmoid(u_gate)
    base = CONV_HIST - (CONV_K - 1)
    groups = [[o for o in range(base, base + CONV_K) if o % SUBLANES == r] for r in range(SUBLANES)]
    for r in range(1, SUBLANES):
        offs = groups[r]
        n_rows = ts + offs[-1] - offs[0]
        shift_ref[r, 0:n_rows, :] = cbuf_ref[pl.ds(offs[0], n_rows), :]
    for rc in range(ts // CONV_ROWS):
        acc = jnp.broadcast_to(dwb_ref[...], (CONV_ROWS, CONV_CH))
        for r in range(SUBLANES):
            for o in groups[r]:
                if r == 0:
                    tap = cbuf_ref[pl.ds(rc * CONV_ROWS + o, CONV_ROWS), :]
                else:
                    tap = shift_ref[r, pl.ds(rc * CONV_ROWS + o - groups[r][0], CONV_ROWS), :]
                acc = acc + dww_ref[o - base:o - base + 1, :] * tap
        conv_ref[rc * CONV_ROWS:(rc + 1) * CONV_ROWS, :] = acc
    cbuf_ref[0:CONV_HIST, :] = cbuf_ref[ts:ts + CONV_HIST, :]
    cv = conv_ref[...]
    d = cv - jnp.mean(cv, axis=-1, keepdims=True)
    ln = d * lax.rsqrt(jnp.mean(d * d, axis=-1, keepdims=True) + LN_EPS) * lng_ref[...] + lnb_ref[...]
    yconv = (ln * jax.nn.sigmoid(ln)).astype(BF16)
    b_out = jnp.dot(yconv, wconv_ref[...], preferred_element_type=F32) + bconv_ref[...]

    p3 = jnp.dot(h, win_ref[:, c1:], preferred_element_type=F32)
    dm = x.shape[1]
    mixed = jax.nn.sigmoid(p3[:, 0:dm]) * a_out + jax.nn.sigmoid(p3[:, dm:2 * dm]) * b_out
    o_ref[...] = x + jnp.dot(mixed.astype(BF16), wout_ref[...], preferred_element_type=F32)


def _mixer_tables(seq, ts):
    half = RET_QK_DIM // 2
    inv = 1.0 / (ROPE_BASE ** (np.arange(half, dtype=np.float64) / half))
    ang = np.arange(seq, dtype=np.float64)[:, None] * inv[None, :]
    cos = np.tile(np.concatenate([np.cos(ang), np.cos(ang)], axis=1), (1, RET_HEADS))
    sin = np.tile(np.concatenate([-np.sin(ang), np.sin(ang)], axis=1), (1, RET_HEADS))
    log_gamma = np.log1p(-(2.0 ** (-5.0 - np.arange(RET_HEADS, dtype=np.float64))))
    n = np.arange(ts, dtype=np.float64)
    diff = n[:, None] - n[None, :]
    intra = np.where(diff[None] >= 0, np.exp(log_gamma[:, None, None] * np.maximum(diff, 0.0)[None]), 0.0)
    q_decay = np.exp(log_gamma[:, None] * (n + 1.0)[None])
    k_decay = np.exp(log_gamma[:, None] * (ts - 1.0 - n)[None])
    chunk_decay = np.exp(log_gamma * ts)
    qd = np.repeat(q_decay.T, RET_V_DIM, axis=1)
    kd = np.repeat(k_decay.T, RET_QK_DIM, axis=1)
    cd = np.repeat(chunk_decay, RET_V_DIM)[None, :]
    row_head = np.arange(RET_QK) // RET_QK_DIM
    col_head = np.arange(RET_WIDTH) // RET_V_DIM
    bd = (row_head[:, None] == col_head[None, :]).astype(np.float64)
    hm = (np.arange(RET_HEADS)[:, None] == row_head[None, :]).astype(np.float64)
    as32 = lambda a: jnp.asarray(a, dtype=F32)
    return tuple(as32(a) for a in (cos, sin, qd, kd, intra, cd, bd, hm))


def _mixer_tile(seq):
    for ts in (256, 128, 64):
        if seq % ts == 0:
            return ts
    raise ValueError(f"sequence length {seq} must be a multiple of {CONV_ROWS}")


def _mixer(x, g, w_in, gn_g, dw_w, dw_b, ln_g, ln_b, w_ret, w_conv, b_conv, w_out):
    bsz, seq, dm = x.shape
    ts = _mixer_tile(seq)
    cos, sin, qd, kd, intra, cd, bd, hm = _mixer_tables(seq, ts)
    row = lambda a: a.reshape(1, -1).astype(F32)
    in_cols = w_in.shape[1]
    grid = (bsz, seq // ts)
    in_specs = [
        pl.BlockSpec((None, ts, dm), lambda b, j: (b, j, 0)),
        _const_spec((1, dm)),
        _const_spec((dm, in_cols)),
        pl.BlockSpec((ts, RET_QK), lambda b, j: (j, 0)),
        pl.BlockSpec((ts, RET_QK), lambda b, j: (j, 0)),
        _const_spec((ts, RET_WIDTH)),
        _const_spec((ts, RET_QK)),
        _const_spec((RET_HEADS, ts, ts)),
        _const_spec((1, RET_WIDTH)),
        _const_spec((RET_QK, RET_WIDTH)),
        _const_spec((RET_HEADS, RET_QK)),
        _const_spec((1, RET_WIDTH)),
        _const_spec((CONV_K, CONV_CH)),
        _const_spec((1, CONV_CH)),
        _const_spec((1, CONV_CH)),
        _const_spec((1, CONV_CH)),
        _const_spec((RET_WIDTH, dm)),
        _const_spec((CONV_CH, dm)),
        _const_spec((1, dm)),
        _const_spec((dm, dm)),
    ]
    return pl.pallas_call(
        functools.partial(_mixer_kernel, ts=ts),
        name="mixer",
        grid=grid,
        in_specs=in_specs,
        out_specs=pl.BlockSpec((None, ts, dm), lambda b, j: (b, j, 0)),
        out_shape=jax.ShapeDtypeStruct((bsz, seq, dm), F32),
        scratch_shapes=[
            pltpu.VMEM((RET_QK, RET_WIDTH), F32),
            pltpu.VMEM((CONV_HIST + ts, CONV_CH), F32),
            pltpu.VMEM((ts, CONV_CH), F32),
            pltpu.VMEM((SUBLANES, ts + CONV_HIST - SUBLANES, CONV_CH), F32),
        ],
        compiler_params=pltpu.CompilerParams(
            dimension_semantics=("arbitrary", "arbitrary"), vmem_limit_bytes=VMEM_LIMIT_BYTES),
    )(x, row(g), w_in.astype(BF16), cos, sin, qd, kd, intra, cd, bd, hm, row(gn_g), dw_w.astype(F32),
      row(dw_b), row(ln_g), row(ln_b), w_ret.astype(BF16), w_conv.astype(BF16), row(b_conv),
      w_out.astype(BF16))


def _top16(s, row_id, break_ties):
    cur = s
    rank = jnp.full(s.shape, NOT_SELECTED, F32)
    vals = []
    for r in range(PEER_TOPK):
        m = jnp.max(cur, axis=0, keepdims=True)
        if break_ties:
            first = jnp.min(jnp.where(cur == m, row_id, 1e9), axis=0, keepdims=True)
            hit = row_id == first
        else:
            hit = cur == m
        rank = jnp.where(hit, float(r), rank)
        cur = jnp.where(hit, -jnp.inf, cur)
        vals.append(m)
    return jnp.concatenate(vals, axis=0), rank


def _staircase(v1, v2, e1x, e2x, break_ties):
    k = PEER_TOPK
    row = lax.broadcasted_iota(jnp.int32, (SUBLANES, LANES), 0).astype(F32)
    ninf = jnp.full((SUBLANES, LANES), -jnp.inf, F32)
    cand, pos, wgt = [], [], []
    for b0 in (0, SUBLANES):
        cand.append(v1[0:1] + v2[b0:b0 + SUBLANES]); pos.append(row + float(b0))
        wgt.append(e1x[0:1] * e2x[b0:b0 + SUBLANES])
    cand.append(v1[1:2] + v2[0:SUBLANES]); pos.append(row + float(k)); wgt.append(e1x[1:2] * e2x[0:SUBLANES])
    cand.append(v1[SUBLANES:k] + v2[0:1]); pos.append((row + float(SUBLANES)) * float(k))
    wgt.append(e1x[SUBLANES:k] * e2x[0:1])
    n_col = 5
    for b in range(n_col):
        a_max = float(k // (b + 1) - 1)
        ok = (row >= 2.0) & (row <= a_max)
        cand.append(jnp.where(ok, v1[0:SUBLANES] + v2[b:b + 1], ninf))
        pos.append(jnp.where(ok, row * float(k) + float(b), -1.0))
        wgt.append(e1x[0:SUBLANES] * e2x[b:b + 1])
    ng = len(cand)
    sel = [jnp.zeros((SUBLANES, LANES), F32) for _ in range(ng)]
    for _ in range(k):
        m = cand[0]
        for g in range(1, ng):
            m = jnp.maximum(m, cand[g])
        m = jnp.max(m, axis=0, keepdims=True)
        if break_ties:
            fp = jnp.where(cand[0] == m, pos[0], 1e9)
            for g in range(1, ng):
                fp = jnp.minimum(fp, jnp.where(cand[g] == m, pos[g], 1e9))
            fp = jnp.min(fp, axis=0, keepdims=True)
        for g in range(ng):
            hit = (pos[g] == fp) if break_ties else (cand[g] == m)
            sel[g] = jnp.where(hit, 1.0, sel[g])
            cand[g] = jnp.where(hit, -jnp.inf, cand[g])
    b0cnt = jnp.sum(sel[0] + sel[1], axis=0, keepdims=True)
    b1cnt = jnp.sum(sel[2], axis=0, keepdims=True)
    low = sel[4]
    for g in range(5, ng):
        low = low + sel[g]
    low = low + jnp.where(row == 0.0, b0cnt, 0.0) + jnp.where(row == 1.0, b1cnt, 0.0)
    counts = jnp.concatenate([low, sel[3]], axis=0)
    zt = sel[0] * wgt[0]
    nsel = sel[0]
    for g in range(1, ng):
        zt = zt + sel[g] * wgt[g]
        nsel = nsel + sel[g]
    return counts, jnp.sum(zt, axis=0, keepdims=True), jnp.sum(nsel, axis=0, keepdims=True)


def _peer_kernel(x_ref, g_ref, wq_ref, keys_ref, u0_ref, u_ref, vt_ref, gfin_ref, o_ref,
                 hn_ref, s_ref, rank2_ref, w2_ref, bcnt_ref, w1_ref, st_ref, at_ref, acc_ref,
                 *, tt, eb, final_norm):
    j = pl.program_id(1)
    n_blk = pl.num_programs(1) - 1
    nch = tt // PEER_CHUNK
    lpc = PEER_CHUNK // LANES
    qd = 2 * PEER_HALF

    @pl.when(j == 0)
    def _routing():
        x = x_ref[...]
        hn = x * lax.rsqrt(jnp.mean(x * x, axis=-1, keepdims=True) + RMS_EPS) * g_ref[...]
        for c in range(nch):
            hn_c = hn[c * PEER_CHUNK:(c + 1) * PEER_CHUNK, :].T.astype(BF16)
            hn_ref[c] = hn_c
            acc_ref[c] = jnp.zeros(acc_ref.shape[1:], F32)
            at_ref[1, c] = jnp.zeros(at_ref.shape[2:], BF16)
            st_ref[0, c] = jnp.dot(u0_ref[...], hn_c, preferred_element_type=F32)
            for hh in range(PEER_HEADS):
                qt = jnp.dot(wq_ref[hh * qd:(hh + 1) * qd, :], hn_c, preferred_element_type=F32).astype(BF16)
                for p in range(2):
                    sc = jnp.dot(keys_ref[2 * hh + p], qt[p * PEER_HALF:(p + 1) * PEER_HALF, :],
                                 preferred_element_type=F32)
                    for l in range(lpc):
                        s_ref[2 * hh + p, c * lpc + l] = sc[:, l * LANES:(l + 1) * LANES]

        row_id = lax.broadcasted_iota(jnp.int32, (N_KEYS, LANES), 0).astype(F32)

        def route(i, carry):
            hh = i // (nch * lpc)
            lb = i % (nch * lpc)
            s1 = s_ref[2 * hh, lb]
            s2 = s_ref[2 * hh + 1, lb]

            def solve(break_ties):
                v1, rank1 = _top16(s1, row_id, break_ties)
                v2, rank2 = _top16(s2, row_id, break_ties)
                e1x = jnp.exp(v1 - v1[0:1])
                e2x = jnp.exp(v2 - v2[0:1])
                counts, z, nsel = _staircase(v1, v2, e1x, e2x, break_ties)
                bcnt = jnp.zeros((N_KEYS, LANES), F32)
                for a in range(PEER_TOPK):
                    bcnt = jnp.where(rank1 == float(a), counts[a:a + 1], bcnt)
                bcnt_ref[hh, lb] = bcnt
                w1_ref[hh, lb] = jnp.exp(s1 - v1[0:1]) * (0.5 / z)
                w2_ref[hh, lb] = jnp.exp(s2 - v2[0:1]).astype(BF16)
                rank2_ref[hh, lb] = rank2.astype(BF16)
                ranked = jnp.where(rank1 < float(PEER_TOPK), 1.0, 0.0) + jnp.where(rank2 < float(PEER_TOPK), 1.0, 0.0)
                return jnp.sum(ranked, axis=0, keepdims=True) + nsel

            picked = solve(break_ties=False)
            tied = jnp.max(jnp.where(picked == float(3 * PEER_TOPK), 0.0, 1.0))

            @pl.when(tied > 0.0)
            def _():
                solve(break_ties=True)

            return carry

        lax.fori_loop(0, PEER_HEADS * nch * lpc, route, 0)

    def steady(cur, nxt):
        n_part = 2
        epp = eb // n_part
        dm = acc_ref.shape[1]

        def part(t, carry):
            c = t // n_part
            hf = t % n_part
            d0 = pl.multiple_of(hf * (dm // n_part), dm // n_part)
            r0 = pl.multiple_of(hf * (epp * N_KEYS), epp * N_KEYS)
            acc_ref[c, pl.ds(d0, dm // n_part), :] += jnp.dot(
                vt_ref[pl.ds(d0, dm // n_part), :], at_ref[nxt, c], preferred_element_type=F32)
            for e in range(epp):
                e1 = j * eb + hf * epp + e
                for l in range(lpc):
                    lb = c * lpc + l
                    cols = slice(l * LANES, (l + 1) * LANES)
                    cnt = [jnp.broadcast_to(bcnt_ref[hh, lb, pl.ds(e1, 1), :].astype(BF16), (BUILD_ROWS, LANES))
                           for hh in range(PEER_HEADS)]
                    w1 = [jnp.broadcast_to(w1_ref[hh, lb, pl.ds(e1, 1), :].astype(BF16), (BUILD_ROWS, LANES))
                          for hh in range(PEER_HEADS)]
                    for g in range(N_KEYS // BUILD_ROWS):
                        grow = slice(g * BUILD_ROWS, (g + 1) * BUILD_ROWS)
                        rows = pl.ds(r0 + e * N_KEYS + g * BUILD_ROWS, BUILD_ROWS)
                        xs = st_ref[cur, c, rows, cols]
                        ge = xs * (1.0 + lax.erf(xs * INV_SQRT2))
                        gate = jnp.zeros((BUILD_ROWS, LANES), BF16)
                        for hh in range(PEER_HEADS):
                            w2 = w2_ref[hh, lb, grow, :]
                            gate = gate + jnp.where(rank2_ref[hh, lb, grow, :] < cnt[hh], w2,
                                                    jnp.zeros_like(w2)) * w1[hh]
                        at_ref[cur, c, rows, cols] = ge.astype(BF16) * gate
            st_ref[nxt, c, pl.ds(r0, epp * N_KEYS), :] = jnp.dot(
                u_ref[pl.ds(r0, epp * N_KEYS), :], hn_ref[c], preferred_element_type=F32)
            return carry

        lax.fori_loop(0, nch * n_part, part, 0)

    @pl.when((j < n_blk) & (j % 2 == 0))
    def _even():
        steady(0, 1)

    @pl.when((j < n_blk) & (j % 2 == 1))
    def _odd():
        steady(1, 0)

    @pl.when(j == n_blk)
    def _finish():
        last = (N_KEYS // eb - 1) % 2
        for c in range(nch):
            acc = acc_ref[c] + jnp.dot(vt_ref[...], at_ref[last, c], preferred_element_type=F32)
            rows = slice(c * PEER_CHUNK, (c + 1) * PEER_CHUNK)
            y = acc.T + x_ref[rows, :]
            if final_norm:
                y = y * lax.rsqrt(jnp.mean(y * y, axis=-1, keepdims=True) + RMS_EPS) * gfin_ref[...]
            o_ref[rows, :] = y


def _peer_tiles(tokens):
    for tt in (512, 256):
        if tokens % tt == 0:
            return tt, 8
    raise ValueError(f"token count {tokens} must be a multiple of {PEER_CHUNK}")


def _peer(x, g, w_query, sub_keys, u_tab, v_tab, g_final, final_norm):
    tokens, dm = x.shape
    tt, eb = _peer_tiles(tokens)
    nlb = tt // LANES
    nch = tt // PEER_CHUNK
    eblk = eb * N_KEYS
    n_exp = u_tab.shape[0]
    assert n_exp == N_KEYS * N_KEYS and N_KEYS % eb == 0
    assert sub_keys.shape == (PEER_HEADS, 2, N_KEYS, PEER_HALF)
    n_blk = N_KEYS // eb
    wq_t = w_query.T.astype(BF16)
    keys = sub_keys.reshape(PEER_HEADS * 2, N_KEYS, PEER_HALF).astype(BF16)
    u_bf = u_tab.astype(BF16)
    vt_bf = v_tab.T.astype(BF16)
    row = lambda a: a.reshape(1, -1).astype(F32)
    grid = (tokens // tt, n_blk + 1)
    return pl.pallas_call(
        functools.partial(_peer_kernel, tt=tt, eb=eb, final_norm=final_norm),
        name="peer",
        grid=grid,
        in_specs=[
            pl.BlockSpec((tt, dm), lambda i, j: (i, 0)),
            _const_spec((1, dm)),
            _const_spec(wq_t.shape),
            _const_spec(keys.shape),
            _const_spec((eblk, dm)),
            pl.BlockSpec((eblk, dm), lambda i, j: (jnp.minimum(j + 1, n_blk - 1), 0)),
            pl.BlockSpec((dm, eblk), lambda i, j: (0, jnp.clip(j - 1, 0, n_blk - 1))),
            _const_spec((1, dm)),
        ],
        out_specs=pl.BlockSpec((tt, dm), lambda i, j: (i, 0)),
        out_shape=jax.ShapeDtypeStruct((tokens, dm), F32),
        scratch_shapes=[
            pltpu.VMEM((nch, dm, PEER_CHUNK), BF16),
            pltpu.VMEM((2 * PEER_HEADS, nlb, N_KEYS, LANES), F32),
            pltpu.VMEM((PEER_HEADS, nlb, N_KEYS, LANES), BF16),
            pltpu.VMEM((PEER_HEADS, nlb, N_KEYS, LANES), BF16),
            pltpu.VMEM((PEER_HEADS, nlb, N_KEYS, LANES), F32),
            pltpu.VMEM((PEER_HEADS, nlb, N_KEYS, LANES), F32),
            pltpu.VMEM((2, nch, eblk, PEER_CHUNK), F32),
            pltpu.VMEM((2, nch, eblk, PEER_CHUNK), BF16),
            pltpu.VMEM((nch, dm, PEER_CHUNK), F32),
        ],
        compiler_params=pltpu.CompilerParams(
            dimension_semantics=("arbitrary", "arbitrary"), vmem_limit_bytes=VMEM_LIMIT_BYTES),
    )(x, row(g), wq_t, keys, u_bf, u_bf, vt_bf, row(g_final))


def kernel(x, rms_mix_g, w_in, ret_gn_g, conv_dw_w, conv_dw_b, conv_ln_g, conv_ln_b, w_ret_proj, w_conv_proj,
           b_conv_proj, w_out, rms_ffn_g, w_query, peer_sub_keys, peer_u, peer_v, rms_final_g):
    bsz, seq, dm = x.shape
    depth = w_in.shape[0]
    for l in range(depth):
        x = _mixer(x, rms_mix_g[l], w_in[l], ret_gn_g[l], conv_dw_w[l], conv_dw_b[l], conv_ln_g[l],
                   conv_ln_b[l], w_ret_proj[l], w_conv_proj[l], b_conv_proj[l], w_out[l])
        x = _peer(x.reshape(bsz * seq, dm), rms_ffn_g[l], w_query[l], peer_sub_keys[l], peer_u[l], peer_v[l],
                  rms_final_g, final_norm=(l == depth - 1)).reshape(bsz, seq, dm)
    return x
```

```python
import functools

import numpy as np
import jax
import jax.numpy as jnp
from jax import lax
from jax.experimental import pallas as pl
from jax.experimental.pallas import tpu as pltpu

F32 = jnp.float32
BF16 = jnp.bfloat16

RET_HEADS = 4
RET_QK_DIM = 64
RET_V_DIM = 128
RET_QK = RET_HEADS * RET_QK_DIM
RET_WIDTH = RET_HEADS * RET_V_DIM
ROPE_BASE = 10000.0
CONV_CH = 512
CONV_K = 31
PEER_HEADS = 8
PEER_HALF = 128
N_KEYS = 128
PEER_TOPK = 16
RMS_EPS = 1e-6
LN_EPS = 1e-5
INV_SQRT2 = 0.7071067811865476

LANES = 128
SUBLANES = 8
VMEM_LIMIT_BYTES = 56 * 1024 * 1024
CONV_HIST = 32
CONV_ROWS = 32
PEER_CHUNK = 256
BUILD_ROWS = 32
NOT_SELECTED = 127.0


def _const_spec(shape):
    nd = len(shape)
    return pl.BlockSpec(shape, lambda *_: (0,) * nd, pipeline_mode=pl.Buffered(1))


def _swap_halves(t):
    half = RET_QK_DIM // 2
    parts = []
    for c in range(t.shape[1] // LANES):
        tc = t[:, c * LANES:(c + 1) * LANES]
        lane = lax.broadcasted_iota(jnp.int32, tc.shape, 1)
        first = (lane % RET_QK_DIM) < half
        nxt = pltpu.roll(tc, LANES - half, 1)
        prv = pltpu.roll(tc, half, 1)
        parts.append(jnp.where(first, nxt, prv))
    return jnp.concatenate(parts, axis=1)


def _mixer_kernel(x_ref, g_ref, win_ref, cos_ref, sin_ref, qd_ref, kd_ref, intra_ref, cd_ref, bd_ref,
                  hm_ref, gn_ref, dww_ref, dwb_ref, lng_ref, lnb_ref, wret_ref, wconv_ref, bconv_ref,
                  wout_ref, o_ref, state_ref, cbuf_ref, conv_ref, shift_ref, *, ts):
    @pl.when(pl.program_id(1) == 0)
    def _():
        state_ref[...] = jnp.zeros_like(state_ref)
        cbuf_ref[0:CONV_HIST, :] = jnp.zeros((CONV_HIST, CONV_CH), F32)

    x = x_ref[...]
    h = (x * lax.rsqrt(jnp.mean(x * x, axis=-1, keepdims=True) + RMS_EPS) * g_ref[...]).astype(BF16)

    c0 = 2 * RET_QK + RET_WIDTH
    qkv = jnp.dot(h, win_ref[:, 0:c0], preferred_element_type=F32)
    q = qkv[:, 0:RET_QK]
    k = qkv[:, RET_QK:2 * RET_QK]
    vb = qkv[:, 2 * RET_QK:c0].astype(BF16)
    cos = cos_ref[...]
    sin = sin_ref[...]
    qr = q * cos + _swap_halves(q) * sin
    kr = (k * cos + _swap_halves(k) * sin) * (RET_QK_DIM ** -0.5)
    qb = qr.astype(BF16)
    kbt = kr.T.astype(BF16)
    inner = []
    for hh in range(RET_HEADS):
        qm = (qr * hm_ref[hh:hh + 1, :]).astype(BF16)
        sc = jnp.dot(qm, kbt, preferred_element_type=F32)
        sc = sc * intra_ref[hh]
        inner.append(jnp.dot(sc.astype(BF16), vb[:, hh * RET_V_DIM:(hh + 1) * RET_V_DIM],
                             preferred_element_type=F32))
    st = state_ref[...]
    cross = jnp.dot(qb, st.astype(BF16), preferred_element_type=F32) * qd_ref[...]
    o = jnp.concatenate(inner, axis=1) + cross
    kv = lax.dot_general((kr * kd_ref[...]).astype(BF16), vb, (((0,), (0,)), ((), ())),
                         preferred_element_type=F32)
    state_ref[...] = st * cd_ref[...] + kv * bd_ref[...]

    normed = []
    for hh in range(RET_HEADS):
        oh = o[:, hh * RET_V_DIM:(hh + 1) * RET_V_DIM]
        d = oh - jnp.mean(oh, axis=-1, keepdims=True)
        normed.append(d * lax.rsqrt(jnp.mean(d * d, axis=-1, keepdims=True) + LN_EPS))
    on = jnp.concatenate(normed, axis=1) * gn_ref[...]

    c1 = c0 + RET_WIDTH + 2 * CONV_CH
    p2 = jnp.dot(h, win_ref[:, c0:c1], preferred_element_type=F32)
    g_ret = p2[:, 0:RET_WIDTH]
    yret = (g_ret * jax.nn.sigmoid(g_ret) * on).astype(BF16)
    a_out = jnp.dot(yret, wret_ref[...], preferred_element_type=F32)

    u_val = p2[:, RET_WIDTH:RET_WIDTH + CONV_CH]
    u_gate = p2[:, RET_WIDTH + CONV_CH:RET_WIDTH + 2 * CONV_CH]
    cbuf_ref[CONV_HIST:CONV_HIST + ts, :] = u_val * jax.nn.sigmoid(u_gate)
    base = CONV_HIST - (CONV_K - 1)
    groups = [[o for o in range(base, base + CONV_K) if o % SUBLANES == r] for r in range(SUBLANES)]
    for r in range(1, SUBLANES):
        offs = groups[r]
        n_rows = ts + offs[-1] - offs[0]
        shift_ref[r, 0:n_rows, :] = cbuf_ref[pl.ds(offs[0], n_rows), :]
    for rc in range(ts // CONV_ROWS):
        acc = jnp.broadcast_to(dwb_ref[...], (CONV_ROWS, CONV_CH))
        for r in range(SUBLANES):
            for o in groups[r]:
                if r == 0:
                    tap = cbuf_ref[pl.ds(rc * CONV_ROWS + o, CONV_ROWS), :]
                else:
                    tap = shift_ref[r, pl.ds(rc * CONV_ROWS + o - groups[r][0], CONV_ROWS), :]
                acc = acc + dww_ref[o - base:o - base + 1, :] * tap
        conv_ref[rc * CONV_ROWS:(rc + 1) * CONV_ROWS, :] = acc
    cbuf_ref[0:CONV_HIST, :] = cbuf_ref[ts:ts + CONV_HIST, :]
    cv = conv_ref[...]
    d = cv - jnp.mean(cv, axis=-1, keepdims=True)
    ln = d * lax.rsqrt(jnp.mean(d * d, axis=-1, keepdims=True) + LN_EPS) * lng_ref[...] + lnb_ref[...]
    yconv = (ln * jax.nn.sigmoid(ln)).astype(BF16)
    b_out = jnp.dot(yconv, wconv_ref[...], preferred_element_type=F32) + bconv_ref[...]

    p3 = jnp.dot(h, win_ref[:, c1:], preferred_element_type=F32)
    dm = x.shape[1]
    mixed = jax.nn.sigmoid(p3[:, 0:dm]) * a_out + jax.nn.sigmoid(p3[:, dm:2 * dm]) * b_out
    o_ref[...] = x + jnp.dot(mixed.astype(BF16), wout_ref[...], preferred_element_type=F32)


def _mixer_tables(seq, ts):
    half = RET_QK_DIM // 2
    inv = 1.0 / (ROPE_BASE ** (np.arange(half, dtype=np.float64) / half))
    ang = np.arange(seq, dtype=np.float64)[:, None] * inv[None, :]
    cos = np.tile(np.concatenate([np.cos(ang), np.cos(ang)], axis=1), (1, RET_HEADS))
    sin = np.tile(np.concatenate([-np.sin(ang), np.sin(ang)], axis=1), (1, RET_HEADS))
    log_gamma = np.log1p(-(2.0 ** (-5.0 - np.arange(RET_HEADS, dtype=np.float64))))
    n = np.arange(ts, dtype=np.float64)
    diff = n[:, None] - n[None, :]
    intra = np.where(diff[None] >= 0, np.exp(log_gamma[:, None, None] * np.maximum(diff, 0.0)[None]), 0.0)
    q_decay = np.exp(log_gamma[:, None] * (n + 1.0)[None])
    k_decay = np.exp(log_gamma[:, None] * (ts - 1.0 - n)[None])
    chunk_decay = np.exp(log_gamma * ts)
    qd = np.repeat(q_decay.T, RET_V_DIM, axis=1)
    kd = np.repeat(k_decay.T, RET_QK_DIM, axis=1)
    cd = np.repeat(chunk_decay, RET_V_DIM)[None, :]
    row_head = np.arange(RET_QK) // RET_QK_DIM
    col_head = np.arange(RET_WIDTH) // RET_V_DIM
    bd = (row_head[:, None] == col_head[None, :]).astype(np.float64)
    hm = (np.arange(RET_HEADS)[:, None] == row_head[None, :]).astype(np.float64)
    as32 = lambda a: jnp.asarray(a, dtype=F32)
    return tuple(as32(a) for a in (cos, sin, qd, kd, intra, cd, bd, hm))


def _mixer_tile(seq):
    for ts in (256, 128, 64):
        if seq % ts == 0:
            return ts
    raise ValueError(f"sequence length {seq} must be a multiple of {CONV_ROWS}")


def _mixer(x, g, w_in, gn_g, dw_w, dw_b, ln_g, ln_b, w_ret, w_conv, b_conv, w_out):
    bsz, seq, dm = x.shape
    ts = _mixer_tile(seq)
    cos, sin, qd, kd, intra, cd, bd, hm = _mixer_tables(seq, ts)
    row = lambda a: a.reshape(1, -1).astype(F32)
    in_cols = w_in.shape[1]
    grid = (bsz, seq // ts)
    in_specs = [
        pl.BlockSpec((None, ts, dm), lambda b, j: (b, j, 0)),
        _const_spec((1, dm)),
        _const_spec((dm, in_cols)),
        pl.BlockSpec((ts, RET_QK), lambda b, j: (j, 0)),
        pl.BlockSpec((ts, RET_QK), lambda b, j: (j, 0)),
        _const_spec((ts, RET_WIDTH)),
        _const_spec((ts, RET_QK)),
        _const_spec((RET_HEADS, ts, ts)),
        _const_spec((1, RET_WIDTH)),
        _const_spec((RET_QK, RET_WIDTH)),
        _const_spec((RET_HEADS, RET_QK)),
        _const_spec((1, RET_WIDTH)),
        _const_spec((CONV_K, CONV_CH)),
        _const_spec((1, CONV_CH)),
        _const_spec((1, CONV_CH)),
        _const_spec((1, CONV_CH)),
        _const_spec((RET_WIDTH, dm)),
        _const_spec((CONV_CH, dm)),
        _const_spec((1, dm)),
        _const_spec((dm, dm)),
    ]
    return pl.pallas_call(
        functools.partial(_mixer_kernel, ts=ts),
        name="mixer",
        grid=grid,
        in_specs=in_specs,
        out_specs=pl.BlockSpec((None, ts, dm), lambda b, j: (b, j, 0)),
        out_shape=jax.ShapeDtypeStruct((bsz, seq, dm), F32),
        scratch_shapes=[
            pltpu.VMEM((RET_QK, RET_WIDTH), F32),
            pltpu.VMEM((CONV_HIST + ts, CONV_CH), F32),
            pltpu.VMEM((ts, CONV_CH), F32),
            pltpu.VMEM((SUBLANES, ts + CONV_HIST - SUBLANES, CONV_CH), F32),
        ],
        compiler_params=pltpu.CompilerParams(
            dimension_semantics=("arbitrary", "arbitrary"), vmem_limit_bytes=VMEM_LIMIT_BYTES),
    )(x, row(g), w_in.astype(BF16), cos, sin, qd, kd, intra, cd, bd, hm, row(gn_g), dw_w.astype(F32),
      row(dw_b), row(ln_g), row(ln_b), w_ret.astype(BF16), w_conv.astype(BF16), row(b_conv),
      w_out.astype(BF16))


def _top16(s, row_id, break_ties):
    cur = s
    rank = jnp.full(s.shape, NOT_SELECTED, F32)
    vals = []
    for r in range(PEER_TOPK):
        m = jnp.max(cur, axis=0, keepdims=True)
        if break_ties:
            first = jnp.min(jnp.where(cur == m, row_id, 1e9), axis=0, keepdims=True)
            hit = row_id == first
        else:
            hit = cur == m
        rank = jnp.where(hit, float(r), rank)
        cur = jnp.where(hit, -jnp.inf, cur)
        vals.append(m)
    return jnp.concatenate(vals, axis=0), rank


def _staircase(v1, v2, e1x, e2x, break_ties):
    k = PEER_TOPK
    row = lax.broadcasted_iota(jnp.int32, (SUBLANES, LANES), 0).astype(F32)
    ninf = jnp.full((SUBLANES, LANES), -jnp.inf, F32)
    cand, pos, wgt = [], [], []
    for b0 in (0, SUBLANES):
        cand.append(v1[0:1] + v2[b0:b0 + SUBLANES]); pos.append(row + float(b0))
        wgt.append(e1x[0:1] * e2x[b0:b0 + SUBLANES])
    cand.append(v1[1:2] + v2[0:SUBLANES]); pos.append(row + float(k)); wgt.append(e1x[1:2] * e2x[0:SUBLANES])
    cand.append(v1[SUBLANES:k] + v2[0:1]); pos.append((row + float(SUBLANES)) * float(k))
    wgt.append(e1x[SUBLANES:k] * e2x[0:1])
    n_col = 5
    for b in range(n_col):
        a_max = float(k // (b + 1) - 1)
        ok = (row >= 2.0) & (row <= a_max)
        cand.append(jnp.where(ok, v1[0:SUBLANES] + v2[b:b + 1], ninf))
        pos.append(jnp.where(ok, row * float(k) + float(b), -1.0))
        wgt.append(e1x[0:SUBLANES] * e2x[b:b + 1])
    ng = len(cand)
    sel = [jnp.zeros((SUBLANES, LANES), F32) for _ in range(ng)]
    for _ in range(k):
        m = cand[0]
        for g in range(1, ng):
            m = jnp.maximum(m, cand[g])
        m = jnp.max(m, axis=0, keepdims=True)
        if break_ties:
            fp = jnp.where(cand[0] == m, pos[0], 1e9)
            for g in range(1, ng):
                fp = jnp.minimum(fp, jnp.where(cand[g] == m, pos[g], 1e9))
            fp = jnp.min(fp, axis=0, keepdims=True)
        for g in range(ng):
            hit = (pos[g] == fp) if break_ties else (cand[g] == m)
            sel[g] = jnp.where(hit, 1.0, sel[g])
            cand[g] = jnp.where(hit, -jnp.inf, cand[g])
    b0cnt = jnp.sum(sel[0] + sel[1], axis=0, keepdims=True)
    b1cnt = jnp.sum(sel[2], axis=0, keepdims=True)
    low = sel[4]
    for g in range(5, ng):
        low = low + sel[g]
    low = low + jnp.where(row == 0.0, b0cnt, 0.0) + jnp.where(row == 1.0, b1cnt, 0.0)
    counts = jnp.concatenate([low, sel[3]], axis=0)
    zt = sel[0] * wgt[0]
    nsel = sel[0]
    for g in range(1, ng):
        zt = zt + sel[g] * wgt[g]
        nsel = nsel + sel[g]
    return counts, jnp.sum(zt, axis=0, keepdims=True), jnp.sum(nsel, axis=0, keepdims=True)


def _peer_kernel(x_ref, g_ref, wq_ref, keys_ref, u0_ref, u_ref, vt_ref, gfin_ref, o_ref,
                 hn_ref, s_ref, rank2_ref, w2_ref, bcnt_ref, w1_ref, st_ref, at_ref, acc_ref,
                 *, tt, eb, final_norm):
    j = pl.program_id(1)
    n_blk = pl.num_programs(1) - 1
    nch = tt // PEER_CHUNK
    lpc = PEER_CHUNK // LANES
    qd = 2 * PEER_HALF

    @pl.when(j == 0)
    def _routing():
        x = x_ref[...]
        hn = x * lax.rsqrt(jnp.mean(x * x, axis=-1, keepdims=True) + RMS_EPS) * g_ref[...]
        for c in range(nch):
            hn_c = hn[c * PEER_CHUNK:(c + 1) * PEER_CHUNK, :].T.astype(BF16)
            hn_ref[c] = hn_c
            acc_ref[c] = jnp.zeros(acc_ref.shape[1:], F32)
            at_ref[1, c] = jnp.zeros(at_ref.shape[2:], BF16)
            st_ref[0, c] = jnp.dot(u0_ref[...], hn_c, preferred_element_type=F32)
            for hh in range(PEER_HEADS):
                qt = jnp.dot(wq_ref[hh * qd:(hh + 1) * qd, :], hn_c, preferred_element_type=F32).astype(BF16)
                for p in range(2):
                    sc = jnp.dot(keys_ref[2 * hh + p], qt[p * PEER_HALF:(p + 1) * PEER_HALF, :],
                                 preferred_element_type=F32)
                    for l in range(lpc):
                        s_ref[2 * hh + p, c * lpc + l] = sc[:, l * LANES:(l + 1) * LANES]

        row_id = lax.broadcasted_iota(jnp.int32, (N_KEYS, LANES), 0).astype(F32)

        def route(i, carry):
            hh = i // (nch * lpc)
            lb = i % (nch * lpc)
            s1 = s_ref[2 * hh, lb]
            s2 = s_ref[2 * hh + 1, lb]

            def solve(break_ties):
                v1, rank1 = _top16(s1, row_id, break_ties)
                v2, rank2 = _top16(s2, row_id, break_ties)
                e1x = jnp.exp(v1 - v1[0:1])
                e2x = jnp.exp(v2 - v2[0:1])
                counts, z, nsel = _staircase(v1, v2, e1x, e2x, break_ties)
                bcnt = jnp.zeros((N_KEYS, LANES), F32)
                for a in range(PEER_TOPK):
                    bcnt = jnp.where(rank1 == float(a), counts[a:a + 1], bcnt)
                bcnt_ref[hh, lb] = bcnt
                w1_ref[hh, lb] = jnp.exp(s1 - v1[0:1]) * (0.5 / z)
                w2_ref[hh, lb] = jnp.exp(s2 - v2[0:1])
                rank2_ref[hh, lb] = rank2
                ranked = jnp.where(rank1 < float(PEER_TOPK), 1.0, 0.0) + jnp.where(rank2 < float(PEER_TOPK), 1.0, 0.0)
                return jnp.sum(ranked, axis=0, keepdims=True) + nsel

            picked = solve(break_ties=False)
            tied = jnp.max(jnp.where(picked == float(3 * PEER_TOPK), 0.0, 1.0))

            @pl.when(tied > 0.0)
            def _():
                solve(break_ties=True)

            return carry

        lax.fori_loop(0, PEER_HEADS * nch * lpc, route, 0)

    def steady(cur, nxt):
        def chunk(c, carry):
            acc_ref[c] += jnp.dot(vt_ref[...], at_ref[nxt, c], preferred_element_type=F32)
            for e in range(eb):
                e1 = j * eb + e
                for l in range(lpc):
                    lb = c * lpc + l
                    cols = slice(l * LANES, (l + 1) * LANES)
                    cnt = [bcnt_ref[hh, lb, pl.ds(e1, 1), :] for hh in range(PEER_HEADS)]
                    w1 = [w1_ref[hh, lb, pl.ds(e1, 1), :] for hh in range(PEER_HEADS)]
                    for g in range(N_KEYS // BUILD_ROWS):
                        grow = slice(g * BUILD_ROWS, (g + 1) * BUILD_ROWS)
                        rows = slice(e * N_KEYS + g * BUILD_ROWS, e * N_KEYS + (g + 1) * BUILD_ROWS)
                        xs = st_ref[cur, c, rows, cols]
                        ge = xs * (1.0 + lax.erf(xs * INV_SQRT2))
                        gate = jnp.zeros((BUILD_ROWS, LANES), F32)
                        for hh in range(PEER_HEADS):
                            gate = gate + jnp.where(rank2_ref[hh, lb, grow, :] < cnt[hh],
                                                    w2_ref[hh, lb, grow, :], 0.0) * w1[hh]
                        at_ref[cur, c, rows, cols] = (ge * gate).astype(BF16)
            st_ref[nxt, c] = jnp.dot(u_ref[...], hn_ref[c], preferred_element_type=F32)
            return carry

        lax.fori_loop(0, nch, chunk, 0)

    @pl.when((j < n_blk) & (j % 2 == 0))
    def _even():
        steady(0, 1)

    @pl.when((j < n_blk) & (j % 2 == 1))
    def _odd():
        steady(1, 0)

    @pl.when(j == n_blk)
    def _finish():
        last = (N_KEYS // eb - 1) % 2
        for c in range(nch):
            acc = acc_ref[c] + jnp.dot(vt_ref[...], at_ref[last, c], preferred_element_type=F32)
            rows = slice(c * PEER_CHUNK, (c + 1) * PEER_CHUNK)
            y = acc.T + x_ref[rows, :]
            if final_norm:
                y = y * lax.rsqrt(jnp.mean(y * y, axis=-1, keepdims=True) + RMS_EPS) * gfin_ref[...]
            o_ref[rows, :] = y


def _peer_tiles(tokens):
    for tt in (512, 256):
        if tokens % tt == 0:
            return tt, 8
    raise ValueError(f"token count {tokens} must be a multiple of {PEER_CHUNK}")


def _peer(x, g, w_query, sub_keys, u_tab, v_tab, g_final, final_norm):
    tokens, dm = x.shape
    tt, eb = _peer_tiles(tokens)
    nlb = tt // LANES
    nch = tt // PEER_CHUNK
    eblk = eb * N_KEYS
    n_exp = u_tab.shape[0]
    assert n_exp == N_KEYS * N_KEYS and N_KEYS % eb == 0
    assert sub_keys.shape == (PEER_HEADS, 2, N_KEYS, PEER_HALF)
    n_blk = N_KEYS // eb
    wq_t = w_query.T.astype(BF16)
    keys = sub_keys.reshape(PEER_HEADS * 2, N_KEYS, PEER_HALF).astype(BF16)
    u_bf = u_tab.astype(BF16)
    vt_bf = v_tab.T.astype(BF16)
    row = lambda a: a.reshape(1, -1).astype(F32)
    grid = (tokens // tt, n_blk + 1)
    return pl.pallas_call(
        functools.partial(_peer_kernel, tt=tt, eb=eb, final_norm=final_norm),
        name="peer",
        grid=grid,
        in_specs=[
            pl.BlockSpec((tt, dm), lambda i, j: (i, 0)),
            _const_spec((1, dm)),
            _const_spec(wq_t.shape),
            _const_spec(keys.shape),
            _const_spec((eblk, dm)),
            pl.BlockSpec((eblk, dm), lambda i, j: (jnp.minimum(j + 1, n_blk - 1), 0)),
            pl.BlockSpec((dm, eblk), lambda i, j: (0, jnp.clip(j - 1, 0, n_blk - 1))),
            _const_spec((1, dm)),
        ],
        out_specs=pl.BlockSpec((tt, dm), lambda i, j: (i, 0)),
        out_shape=jax.ShapeDtypeStruct((tokens, dm), F32),
        scratch_shapes=[
            pltpu.VMEM((nch, dm, PEER_CHUNK), BF16),
            pltpu.VMEM((2 * PEER_HEADS, nlb, N_KEYS, LANES), F32),
            pltpu.VMEM((PEER_HEADS, nlb, N_KEYS, LANES), F32),
            pltpu.VMEM((PEER_HEADS, nlb, N_KEYS, LANES), F32),
            pltpu.VMEM((PEER_HEADS, nlb, N_KEYS, LANES), F32),
            pltpu.VMEM((PEER_HEADS, nlb, N_KEYS, LANES), F32),
            pltpu.VMEM((2, nch, eblk, PEER_CHUNK), F32),
            pltpu.VMEM((2, nch, eblk, PEER_CHUNK), BF16),
            pltpu.VMEM((nch, dm, PEER_CHUNK), F32),
        ],
        compiler_params=pltpu.CompilerParams(
            dimension_semantics=("arbitrary", "arbitrary"), vmem_limit_bytes=VMEM_LIMIT_BYTES),
    )(x, row(g), wq_t, keys, u_bf, u_bf, vt_bf, row(g_final))


def kernel(x, rms_mix_g, w_in, ret_gn_g, conv_dw_w, conv_dw_b, conv_ln_g, conv_ln_b, w_ret_proj, w_conv_proj,
           b_conv_proj, w_out, rms_ffn_g, w_query, peer_sub_keys, peer_u, peer_v, rms_final_g):
    bsz, seq, dm = x.shape
    depth = w_in.shape[0]
    for l in range(depth):
        x = _mixer(x, rms_mix_g[l], w_in[l], ret_gn_g[l], conv_dw_w[l], conv_dw_b[l], conv_ln_g[l],
                   conv_ln_b[l], w_ret_proj[l], w_conv_proj[l], b_conv_proj[l], w_out[l])
        x = _peer(x.reshape(bsz * seq, dm), rms_ffn_g[l], w_query[l], peer_sub_keys[l], peer_u[l], peer_v[l],
                  rms_final_g, final_norm=(l == depth - 1)).reshape(bsz, seq, dm)
    return x
```

```python
import functools

import numpy as np
import jax
import jax.numpy as jnp
from jax import lax
from jax.experimental import pallas as pl
from jax.experimental.pallas import tpu as pltpu

F32 = jnp.float32
BF16 = jnp.bfloat16

RET_HEADS = 4
RET_QK_DIM = 64
RET_V_DIM = 128
RET_QK = RET_HEADS * RET_QK_DIM
RET_WIDTH = RET_HEADS * RET_V_DIM
ROPE_BASE = 10000.0
CONV_CH = 512
CONV_K = 31
PEER_HEADS = 8
PEER_HALF = 128
N_KEYS = 128
PEER_TOPK = 16
RMS_EPS = 1e-6
LN_EPS = 1e-5
INV_SQRT2 = 0.7071067811865476

LANES = 128
SUBLANES = 8
VMEM_LIMIT_BYTES = 56 * 1024 * 1024
CONV_HIST = 32
CONV_ROWS = 32
PEER_CHUNK = 256
BUILD_ROWS = 32
NOT_SELECTED = 127.0


def _const_spec(shape):
    nd = len(shape)
    return pl.BlockSpec(shape, lambda *_: (0,) * nd, pipeline_mode=pl.Buffered(1))


def _swap_halves(t):
    half = RET_QK_DIM // 2
    parts = []
    for c in range(t.shape[1] // LANES):
        tc = t[:, c * LANES:(c + 1) * LANES]
        lane = lax.broadcasted_iota(jnp.int32, tc.shape, 1)
        first = (lane % RET_QK_DIM) < half
        nxt = pltpu.roll(tc, LANES - half, 1)
        prv = pltpu.roll(tc, half, 1)
        parts.append(jnp.where(first, nxt, prv))
    return jnp.concatenate(parts, axis=1)


def _mixer_kernel(x_ref, g_ref, win_ref, cos_ref, sin_ref, qd_ref, kd_ref, intra_ref, cd_ref, bd_ref,
                  hm_ref, gn_ref, dww_ref, dwb_ref, lng_ref, lnb_ref, wret_ref, wconv_ref, bconv_ref,
                  wout_ref, o_ref, state_ref, cbuf_ref, conv_ref, shift_ref, *, ts):
    @pl.when(pl.program_id(1) == 0)
    def _():
        state_ref[...] = jnp.zeros_like(state_ref)
        cbuf_ref[0:CONV_HIST, :] = jnp.zeros((CONV_HIST, CONV_CH), F32)

    x = x_ref[...]
    h = (x * lax.rsqrt(jnp.mean(x * x, axis=-1, keepdims=True) + RMS_EPS) * g_ref[...]).astype(BF16)

    c0 = 2 * RET_QK + RET_WIDTH
    qkv = jnp.dot(h, win_ref[:, 0:c0], preferred_element_type=F32)
    q = qkv[:, 0:RET_QK]
    k = qkv[:, RET_QK:2 * RET_QK]
    vb = qkv[:, 2 * RET_QK:c0].astype(BF16)
    cos = cos_ref[...]
    sin = sin_ref[...]
    qr = q * cos + _swap_halves(q) * sin
    kr = (k * cos + _swap_halves(k) * sin) * (RET_QK_DIM ** -0.5)
    qb = qr.astype(BF16)
    kbt = kr.T.astype(BF16)
    inner = []
    for hh in range(RET_HEADS):
        qm = (qr * hm_ref[hh:hh + 1, :]).astype(BF16)
        sc = jnp.dot(qm, kbt, preferred_element_type=F32)
        sc = sc * intra_ref[hh]
        inner.append(jnp.dot(sc.astype(BF16), vb[:, hh * RET_V_DIM:(hh + 1) * RET_V_DIM],
                             preferred_element_type=F32))
    st = state_ref[...]
    cross = jnp.dot(qb, st.astype(BF16), preferred_element_type=F32) * qd_ref[...]
    o = jnp.concatenate(inner, axis=1) + cross
    kv = lax.dot_general((kr * kd_ref[...]).astype(BF16), vb, (((0,), (0,)), ((), ())),
                         preferred_element_type=F32)
    state_ref[...] = st * cd_ref[...] + kv * bd_ref[...]

    normed = []
    for hh in range(RET_HEADS):
        oh = o[:, hh * RET_V_DIM:(hh + 1) * RET_V_DIM]
        d = oh - jnp.mean(oh, axis=-1, keepdims=True)
        normed.append(d * lax.rsqrt(jnp.mean(d * d, axis=-1, keepdims=True) + LN_EPS))
    on = jnp.concatenate(normed, axis=1) * gn_ref[...]

    c1 = c0 + RET_WIDTH + 2 * CONV_CH
    p2 = jnp.dot(h, win_ref[:, c0:c1], preferred_element_type=F32)
    g_ret = p2[:, 0:RET_WIDTH]
    yret = (g_ret * jax.nn.sigmoid(g_ret) * on).astype(BF16)
    a_out = jnp.dot(yret, wret_ref[...], preferred_element_type=F32)

    u_val = p2[:, RET_WIDTH:RET_WIDTH + CONV_CH]
    u_gate = p2[:, RET_WIDTH + CONV_CH:RET_WIDTH + 2 * CONV_CH]
    cbuf_ref[CONV_HIST:CONV_HIST + ts, :] = u_val * jax.nn.sigmoid(u_gate)
    base = CONV_HIST - (CONV_K - 1)
    groups = [[o for o in range(base, base + CONV_K) if o % SUBLANES == r] for r in range(SUBLANES)]
    for r in range(1, SUBLANES):
        offs = groups[r]
        n_rows = ts + offs[-1] - offs[0]
        shift_ref[r, 0:n_rows, :] = cbuf_ref[pl.ds(offs[0], n_rows), :]
    for rc in range(ts // CONV_ROWS):
        acc = jnp.broadcast_to(dwb_ref[...], (CONV_ROWS, CONV_CH))
        for r in range(SUBLANES):
            for o in groups[r]:
                if r == 0:
                    tap = cbuf_ref[pl.ds(rc * CONV_ROWS + o, CONV_ROWS), :]
                else:
                    tap = shift_ref[r, pl.ds(rc * CONV_ROWS + o - groups[r][0], CONV_ROWS), :]
                acc = acc + dww_ref[o - base:o - base + 1, :] * tap
        conv_ref[rc * CONV_ROWS:(rc + 1) * CONV_ROWS, :] = acc
    cbuf_ref[0:CONV_HIST, :] = cbuf_ref[ts:ts + CONV_HIST, :]
    cv = conv_ref[...]
    d = cv - jnp.mean(cv, axis=-1, keepdims=True)
    ln = d * lax.rsqrt(jnp.mean(d * d, axis=-1, keepdims=True) + LN_EPS) * lng_ref[...] + lnb_ref[...]
    yconv = (ln * jax.nn.sigmoid(ln)).astype(BF16)
    b_out = jnp.dot(yconv, wconv_ref[...], preferred_element_type=F32) + bconv_ref[...]

    p3 = jnp.dot(h, win_ref[:, c1:], preferred_element_type=F32)
    dm = x.shape[1]
    mixed = jax.nn.sigmoid(p3[:, 0:dm]) * a_out + jax.nn.sigmoid(p3[:, dm:2 * dm]) * b_out
    o_ref[...] = x + jnp.dot(mixed.astype(BF16), wout_ref[...], preferred_element_type=F32)


def _mixer_tables(seq, ts):
    half = RET_QK_DIM // 2
    inv = 1.0 / (ROPE_BASE ** (np.arange(half, dtype=np.float64) / half))
    ang = np.arange(seq, dtype=np.float64)[:, None] * inv[None, :]
    cos = np.tile(np.concatenate([np.cos(ang), np.cos(ang)], axis=1), (1, RET_HEADS))
    sin = np.tile(np.concatenate([-np.sin(ang), np.sin(ang)], axis=1), (1, RET_HEADS))
    log_gamma = np.log1p(-(2.0 ** (-5.0 - np.arange(RET_HEADS, dtype=np.float64))))
    n = np.arange(ts, dtype=np.float64)
    diff = n[:, None] - n[None, :]
    intra = np.where(diff[None] >= 0, np.exp(log_gamma[:, None, None] * np.maximum(diff, 0.0)[None]), 0.0)
    q_decay = np.exp(log_gamma[:, None] * (n + 1.0)[None])
    k_decay = np.exp(log_gamma[:, None] * (ts - 1.0 - n)[None])
    chunk_decay = np.exp(log_gamma * ts)
    qd = np.repeat(q_decay.T, RET_V_DIM, axis=1)
    kd = np.repeat(k_decay.T, RET_QK_DIM, axis=1)
    cd = np.repeat(chunk_decay, RET_V_DIM)[None, :]
    row_head = np.arange(RET_QK) // RET_QK_DIM
    col_head = np.arange(RET_WIDTH) // RET_V_DIM
    bd = (row_head[:, None] == col_head[None, :]).astype(np.float64)
    hm = (np.arange(RET_HEADS)[:, None] == row_head[None, :]).astype(np.float64)
    as32 = lambda a: jnp.asarray(a, dtype=F32)
    return tuple(as32(a) for a in (cos, sin, qd, kd, intra, cd, bd, hm))


def _mixer_tile(seq):
    for ts in (256, 128, 64):
        if seq % ts == 0:
            return ts
    raise ValueError(f"sequence length {seq} must be a multiple of {CONV_ROWS}")


def _mixer(x, g, w_in, gn_g, dw_w, dw_b, ln_g, ln_b, w_ret, w_conv, b_conv, w_out):
    bsz, seq, dm = x.shape
    ts = _mixer_tile(seq)
    cos, sin, qd, kd, intra, cd, bd, hm = _mixer_tables(seq, ts)
    row = lambda a: a.reshape(1, -1).astype(F32)
    in_cols = w_in.shape[1]
    grid = (bsz, seq // ts)
    in_specs = [
        pl.BlockSpec((None, ts, dm), lambda b, j: (b, j, 0)),
        _const_spec((1, dm)),
        _const_spec((dm, in_cols)),
        pl.BlockSpec((ts, RET_QK), lambda b, j: (j, 0)),
        pl.BlockSpec((ts, RET_QK), lambda b, j: (j, 0)),
        _const_spec((ts, RET_WIDTH)),
        _const_spec((ts, RET_QK)),
        _const_spec((RET_HEADS, ts, ts)),
        _const_spec((1, RET_WIDTH)),
        _const_spec((RET_QK, RET_WIDTH)),
        _const_spec((RET_HEADS, RET_QK)),
        _const_spec((1, RET_WIDTH)),
        _const_spec((CONV_K, CONV_CH)),
        _const_spec((1, CONV_CH)),
        _const_spec((1, CONV_CH)),
        _const_spec((1, CONV_CH)),
        _const_spec((RET_WIDTH, dm)),
        _const_spec((CONV_CH, dm)),
        _const_spec((1, dm)),
        _const_spec((dm, dm)),
    ]
    return pl.pallas_call(
        functools.partial(_mixer_kernel, ts=ts),
        name="mixer",
        grid=grid,
        in_specs=in_specs,
        out_specs=pl.BlockSpec((None, ts, dm), lambda b, j: (b, j, 0)),
        out_shape=jax.ShapeDtypeStruct((bsz, seq, dm), F32),
        scratch_shapes=[
            pltpu.VMEM((RET_QK, RET_WIDTH), F32),
            pltpu.VMEM((CONV_HIST + ts, CONV_CH), F32),
            pltpu.VMEM((ts, CONV_CH), F32),
            pltpu.VMEM((SUBLANES, ts + CONV_HIST - SUBLANES, CONV_CH), F32),
        ],
        compiler_params=pltpu.CompilerParams(
            dimension_semantics=("arbitrary", "arbitrary"), vmem_limit_bytes=VMEM_LIMIT_BYTES),
    )(x, row(g), w_in.astype(BF16), cos, sin, qd, kd, intra, cd, bd, hm, row(gn_g), dw_w.astype(F32),
      row(dw_b), row(ln_g), row(ln_b), w_ret.astype(BF16), w_conv.astype(BF16), row(b_conv),
      w_out.astype(BF16))


def _zip_steps(*steppers):
    results = [None] * len(steppers)
    live = dict(enumerate(steppers))
    while live:
        for idx in list(live):
            try:
                next(live[idx])
            except StopIteration as done:
                results[idx] = done.value
                del live[idx]
        if live:
            yield
    return results


def _interleave(*steppers):
    zipped = _zip_steps(*steppers)
    try:
        while True:
            next(zipped)
    except StopIteration as done:
        return done.value


def _top16(s, row_id, break_ties, want_rank=True):
    cur = s
    rank = jnp.full(s.shape, NOT_SELECTED, F32) if want_rank else None
    vals = []
    for r in range(PEER_TOPK):
        m = jnp.max(cur, axis=0, keepdims=True)
        if break_ties:
            first = jnp.min(jnp.where(cur == m, row_id, 1e9), axis=0, keepdims=True)
            hit = row_id == first
        else:
            hit = cur == m
        if want_rank:
            rank = jnp.where(hit, float(r), rank)
        cur = jnp.where(hit, -jnp.inf, cur)
        vals.append(m)
        yield
    return jnp.concatenate(vals, axis=0), rank


def _staircase(v1, v2, e1x, e2x, break_ties):
    k = PEER_TOPK
    row = lax.broadcasted_iota(jnp.int32, (SUBLANES, LANES), 0).astype(F32)
    ninf = jnp.full((SUBLANES, LANES), -jnp.inf, F32)
    cand, pos, wgt = [], [], []
    for b0 in (0, SUBLANES):
        cand.append(v1[0:1] + v2[b0:b0 + SUBLANES]); pos.append(row + float(b0))
        wgt.append(e1x[0:1] * e2x[b0:b0 + SUBLANES])
    cand.append(v1[1:2] + v2[0:SUBLANES]); pos.append(row + float(k)); wgt.append(e1x[1:2] * e2x[0:SUBLANES])
    cand.append(v1[SUBLANES:k] + v2[0:1]); pos.append((row + float(SUBLANES)) * float(k))
    wgt.append(e1x[SUBLANES:k] * e2x[0:1])
    n_col = 5
    for b in range(n_col):
        a_max = float(k // (b + 1) - 1)
        ok = (row >= 2.0) & (row <= a_max)
        cand.append(jnp.where(ok, v1[0:SUBLANES] + v2[b:b + 1], ninf))
        pos.append(jnp.where(ok, row * float(k) + float(b), -1.0))
        wgt.append(e1x[0:SUBLANES] * e2x[b:b + 1])
    ng = len(cand)
    sel = [jnp.zeros((SUBLANES, LANES), F32) for _ in range(ng)]
    for _ in range(k):
        m = cand[0]
        for g in range(1, ng):
            m = jnp.maximum(m, cand[g])
        m = jnp.max(m, axis=0, keepdims=True)
        if break_ties:
            fp = jnp.where(cand[0] == m, pos[0], 1e9)
            for g in range(1, ng):
                fp = jnp.minimum(fp, jnp.where(cand[g] == m, pos[g], 1e9))
            fp = jnp.min(fp, axis=0, keepdims=True)
        for g in range(ng):
            hit = (pos[g] == fp) if break_ties else (cand[g] == m)
            sel[g] = jnp.where(hit, 1.0, sel[g])
            cand[g] = jnp.where(hit, -jnp.inf, cand[g])
        yield
    b0cnt = jnp.sum(sel[0] + sel[1], axis=0, keepdims=True)
    b1cnt = jnp.sum(sel[2], axis=0, keepdims=True)
    low = sel[4]
    for g in range(5, ng):
        low = low + sel[g]
    low = low + jnp.where(row == 0.0, b0cnt, 0.0) + jnp.where(row == 1.0, b1cnt, 0.0)
    counts = jnp.concatenate([low, sel[3]], axis=0)
    zt = sel[0] * wgt[0]
    nsel = sel[0]
    for g in range(1, ng):
        zt = zt + sel[g] * wgt[g]
        nsel = nsel + sel[g]
    return counts, jnp.sum(zt, axis=0, keepdims=True), jnp.sum(nsel, axis=0, keepdims=True)


def _peer_kernel(x_ref, g_ref, wq_ref, keys_ref, u0_ref, u_ref, vt_ref, gfin_ref, o_ref,
                 hn_ref, s_ref, vbuf_ref, rank2_ref, w2_ref, bcnt_ref, w1_ref, st_ref, at_ref, acc_ref,
                 *, tt, eb, final_norm):
    j = pl.program_id(1)
    n_blk = pl.num_programs(1) - 1
    nch = tt // PEER_CHUNK
    lpc = PEER_CHUNK // LANES
    qd = 2 * PEER_HALF

    @pl.when(j == 0)
    def _routing():
        x = x_ref[...]
        hn = x * lax.rsqrt(jnp.mean(x * x, axis=-1, keepdims=True) + RMS_EPS) * g_ref[...]
        for c in range(nch):
            hn_c = hn[c * PEER_CHUNK:(c + 1) * PEER_CHUNK, :].T.astype(BF16)
            hn_ref[c] = hn_c
            acc_ref[c] = jnp.zeros(acc_ref.shape[1:], F32)
            at_ref[1, c] = jnp.zeros(at_ref.shape[2:], BF16)
            st_ref[0, c] = jnp.dot(u0_ref[...], hn_c, preferred_element_type=F32)
            for hh in range(PEER_HEADS):
                qt = jnp.dot(wq_ref[hh * qd:(hh + 1) * qd, :], hn_c, preferred_element_type=F32).astype(BF16)
                for p in range(2):
                    sc = jnp.dot(keys_ref[2 * hh + p], qt[p * PEER_HALF:(p + 1) * PEER_HALF, :],
                                 preferred_element_type=F32)
                    for l in range(lpc):
                        s_ref[2 * hh + p, c * lpc + l] = sc[:, l * LANES:(l + 1) * LANES]

        row_id = lax.broadcasted_iota(jnp.int32, (N_KEYS, LANES), 0).astype(F32)

        n_route = PEER_HEADS * nch * lpc
        topk = float(PEER_TOPK)

        def scores(blk):
            hh = blk // (nch * lpc)
            lb = blk % (nch * lpc)
            return hh, lb, s_ref[2 * hh, lb], s_ref[2 * hh + 1, lb]

        def finish(hh, lb, s1, s2, v1, v2, counts, z, is_rank):
            bcnt = jnp.zeros((N_KEYS, LANES), F32)
            for a in range(PEER_TOPK):
                bcnt = jnp.where(is_rank(a), counts[a:a + 1], bcnt)
            bcnt_ref[hh, lb] = bcnt
            w1_ref[hh, lb] = jnp.exp(s1 - v1[0:1]) * (0.5 / z)
            w2_ref[hh, lb] = jnp.exp(s2 - v2[0:1])

        def fast_a(blk, slot):
            hh, lb, s1, s2 = scores(blk)
            (v1, _), (v2, rank2) = yield from _zip_steps(_top16(s1, row_id, False, want_rank=False),
                                                         _top16(s2, row_id, False))
            rank2_ref[hh, lb] = rank2
            ranked = jnp.where(s1 >= v1[PEER_TOPK - 1:PEER_TOPK], 1.0, 0.0) + jnp.where(rank2 < topk, 1.0, 0.0)
            vbuf_ref[slot, 0] = v1
            vbuf_ref[slot, 1] = v2
            vbuf_ref[slot, 2, 0:1, :] = jnp.sum(ranked, axis=0, keepdims=True)

        def fast_b(blk, slot):
            hh, lb, s1, s2 = scores(blk)
            v1 = vbuf_ref[slot, 0]
            v2 = vbuf_ref[slot, 1]
            counts, z, nsel = yield from _staircase(v1, v2, jnp.exp(v1 - v1[0:1]), jnp.exp(v2 - v2[0:1]), False)
            finish(hh, lb, s1, s2, v1, v2, counts, z, lambda a: s1 == v1[a:a + 1])
            return vbuf_ref[slot, 2, 0:1, :] + nsel

        def exact(blk):
            hh, lb, s1, s2 = scores(blk)
            (v1, rank1), (v2, rank2) = _interleave(_top16(s1, row_id, True), _top16(s2, row_id, True))
            rank2_ref[hh, lb] = rank2
            (counts, z, _), = _interleave(
                _staircase(v1, v2, jnp.exp(v1 - v1[0:1]), jnp.exp(v2 - v2[0:1]), True))
            finish(hh, lb, s1, s2, v1, v2, counts, z, lambda a: rank1 == float(a))

        _interleave(fast_a(0, 0))

        def route(i, carry):
            slot = i % 2
            picked, _ = _interleave(fast_b(i, slot), fast_a(jnp.minimum(i + 1, n_route - 1), 1 - slot))
            tied = jnp.max(jnp.where(picked == 3.0 * topk, 0.0, 1.0))

            @pl.when(tied > 0.0)
            def _():
                exact(i)

            return carry

        lax.fori_loop(0, n_route, route, 0)

    def steady(cur, nxt):
        def chunk(c, carry):
            acc_ref[c] += jnp.dot(vt_ref[...], at_ref[nxt, c], preferred_element_type=F32)
            for e in range(eb):
                e1 = j * eb + e
                for l in range(lpc):
                    lb = c * lpc + l
                    cols = slice(l * LANES, (l + 1) * LANES)
                    cnt = [bcnt_ref[hh, lb, pl.ds(e1, 1), :] for hh in range(PEER_HEADS)]
                    w1 = [w1_ref[hh, lb, pl.ds(e1, 1), :] for hh in range(PEER_HEADS)]
                    for g in range(N_KEYS // BUILD_ROWS):
                        grow = slice(g * BUILD_ROWS, (g + 1) * BUILD_ROWS)
                        rows = slice(e * N_KEYS + g * BUILD_ROWS, e * N_KEYS + (g + 1) * BUILD_ROWS)
                        xs = st_ref[cur, c, rows, cols]
                        ge = xs * (1.0 + lax.erf(xs * INV_SQRT2))
                        gate = jnp.zeros((BUILD_ROWS, LANES), F32)
                        for hh in range(PEER_HEADS):
                            gate = gate + jnp.where(rank2_ref[hh, lb, grow, :] < cnt[hh],
                                                    w2_ref[hh, lb, grow, :], 0.0) * w1[hh]
                        at_ref[cur, c, rows, cols] = (ge * gate).astype(BF16)
            st_ref[nxt, c] = jnp.dot(u_ref[...], hn_ref[c], preferred_element_type=F32)
            return carry

        lax.fori_loop(0, nch, chunk, 0)

    @pl.when((j < n_blk) & (j % 2 == 0))
    def _even():
        steady(0, 1)

    @pl.when((j < n_blk) & (j % 2 == 1))
    def _odd():
        steady(1, 0)

    @pl.when(j == n_blk)
    def _finish():
        last = (N_KEYS // eb - 1) % 2
        for c in range(nch):
            acc = acc_ref[c] + jnp.dot(vt_ref[...], at_ref[last, c], preferred_element_type=F32)
            rows = slice(c * PEER_CHUNK, (c + 1) * PEER_CHUNK)
            y = acc.T + x_ref[rows, :]
            if final_norm:
                y = y * lax.rsqrt(jnp.mean(y * y, axis=-1, keepdims=True) + RMS_EPS) * gfin_ref[...]
            o_ref[rows, :] = y


def _peer_tiles(tokens):
    for tt in (512, 256):
        if tokens % tt == 0:
            return tt, 8
    raise ValueError(f"token count {tokens} must be a multiple of {PEER_CHUNK}")


def _peer(x, g, w_query, sub_keys, u_tab, v_tab, g_final, final_norm):
    tokens, dm = x.shape
    tt, eb = _peer_tiles(tokens)
    nlb = tt // LANES
    nch = tt // PEER_CHUNK
    eblk = eb * N_KEYS
    n_exp = u_tab.shape[0]
    assert n_exp == N_KEYS * N_KEYS and N_KEYS % eb == 0
    assert sub_keys.shape == (PEER_HEADS, 2, N_KEYS, PEER_HALF)
    n_blk = N_KEYS // eb
    wq_t = w_query.T.astype(BF16)
    keys = sub_keys.reshape(PEER_HEADS * 2, N_KEYS, PEER_HALF).astype(BF16)
    u_bf = u_tab.astype(BF16)
    vt_bf = v_tab.T.astype(BF16)
    row = lambda a: a.reshape(1, -1).astype(F32)
    grid = (tokens // tt, n_blk + 1)
    return pl.pallas_call(
        functools.partial(_peer_kernel, tt=tt, eb=eb, final_norm=final_norm),
        name="peer",
        grid=grid,
        in_specs=[
            pl.BlockSpec((tt, dm), lambda i, j: (i, 0)),
            _const_spec((1, dm)),
            _const_spec(wq_t.shape),
            _const_spec(keys.shape),
            _const_spec((eblk, dm)),
            pl.BlockSpec((eblk, dm), lambda i, j: (jnp.minimum(j + 1, n_blk - 1), 0)),
            pl.BlockSpec((dm, eblk), lambda i, j: (0, jnp.clip(j - 1, 0, n_blk - 1))),
            _const_spec((1, dm)),
        ],
        out_specs=pl.BlockSpec((tt, dm), lambda i, j: (i, 0)),
        out_shape=jax.ShapeDtypeStruct((tokens, dm), F32),
        scratch_shapes=[
            pltpu.VMEM((nch, dm, PEER_CHUNK), BF16),
            pltpu.VMEM((2 * PEER_HEADS, nlb, N_KEYS, LANES), F32),
            pltpu.VMEM((2, 3, PEER_TOPK, LANES), F32),
            pltpu.VMEM((PEER_HEADS, nlb, N_KEYS, LANES), F32),
            pltpu.VMEM((PEER_HEADS, nlb, N_KEYS, LANES), F32),
            pltpu.VMEM((PEER_HEADS, nlb, N_KEYS, LANES), F32),
            pltpu.VMEM((PEER_HEADS, nlb, N_KEYS, LANES), F32),
            pltpu.VMEM((2, nch, eblk, PEER_CHUNK), F32),
            pltpu.VMEM((2, nch, eblk, PEER_CHUNK), BF16),
            pltpu.VMEM((nch, dm, PEER_CHUNK), F32),
        ],
        compiler_params=pltpu.CompilerParams(
            dimension_semantics=("arbitrary", "arbitrary"), vmem_limit_bytes=VMEM_LIMIT_BYTES),
    )(x, row(g), wq_t, keys, u_bf, u_bf, vt_bf, row(g_final))


def kernel(x, rms_mix_g, w_in, ret_gn_g, conv_dw_w, conv_dw_b, conv_ln_g, conv_ln_b, w_ret_proj, w_conv_proj,
           b_conv_proj, w_out, rms_ffn_g, w_query, peer_sub_keys, peer_u, peer_v, rms_final_g):
    bsz, seq, dm = x.shape
    depth = w_in.shape[0]
    for l in range(depth):
        x = _mixer(x, rms_mix_g[l], w_in[l], ret_gn_g[l], conv_dw_w[l], conv_dw_b[l], conv_ln_g[l],
                   conv_ln_b[l], w_ret_proj[l], w_conv_proj[l], b_conv_proj[l], w_out[l])
        x = _peer(x.reshape(bsz * seq, dm), rms_ffn_g[l], w_query[l], peer_sub_keys[l], peer_u[l], peer_v[l],
                  rms_final_g, final_norm=(l == depth - 1)).reshape(bsz, seq, dm)
    return x
```

```python
import functools

import numpy as np
import jax
import jax.numpy as jnp
from jax import lax
from jax.experimental import pallas as pl
from jax.experimental.pallas import tpu as pltpu

F32 = jnp.float32
BF16 = jnp.bfloat16

RET_HEADS = 4
RET_QK_DIM = 64
RET_V_DIM = 128
RET_QK = RET_HEADS * RET_QK_DIM
RET_WIDTH = RET_HEADS * RET_V_DIM
ROPE_BASE = 10000.0
CONV_CH = 512
CONV_K = 31
PEER_HEADS = 8
PEER_HALF = 128
N_KEYS = 128
PEER_TOPK = 16
RMS_EPS = 1e-6
LN_EPS = 1e-5
INV_SQRT2 = 0.7071067811865476

LANES = 128
SUBLANES = 8
VMEM_LIMIT_BYTES = 56 * 1024 * 1024
CONV_HIST = 32
CONV_ROWS = 32
PEER_CHUNK = 256
BUILD_ROWS = 32
NOT_SELECTED = 127.0


def _const_spec(shape):
    nd = len(shape)
    return pl.BlockSpec(shape, lambda *_: (0,) * nd, pipeline_mode=pl.Buffered(1))


def _swap_halves(t):
    half = RET_QK_DIM // 2
    parts = []
    for c in range(t.shape[1] // LANES):
        tc = t[:, c * LANES:(c + 1) * LANES]
        lane = lax.broadcasted_iota(jnp.int32, tc.shape, 1)
        first = (lane % RET_QK_DIM) < half
        nxt = pltpu.roll(tc, LANES - half, 1)
        prv = pltpu.roll(tc, half, 1)
        parts.append(jnp.where(first, nxt, prv))
    return jnp.concatenate(parts, axis=1)


def _mixer_kernel(x_ref, g_ref, win_ref, cos_ref, sin_ref, qd_ref, kd_ref, intra_ref, cd_ref, bd_ref,
                  hm_ref, gn_ref, dww_ref, dwb_ref, lng_ref, lnb_ref, wret_ref, wconv_ref, bconv_ref,
                  wout_ref, o_ref, state_ref, cbuf_ref, conv_ref, shift_ref, *, ts):
    @pl.when(pl.program_id(1) == 0)
    def _():
        state_ref[...] = jnp.zeros_like(state_ref)
        cbuf_ref[0:CONV_HIST, :] = jnp.zeros((CONV_HIST, CONV_CH), F32)

    x = x_ref[...]
    h = (x * lax.rsqrt(jnp.mean(x * x, axis=-1, keepdims=True) + RMS_EPS) * g_ref[...]).astype(BF16)

    c0 = 2 * RET_QK + RET_WIDTH
    qkv = jnp.dot(h, win_ref[:, 0:c0], preferred_element_type=F32)
    q = qkv[:, 0:RET_QK]
    k = qkv[:, RET_QK:2 * RET_QK]
    vb = qkv[:, 2 * RET_QK:c0].astype(BF16)
    cos = cos_ref[...]
    sin = sin_ref[...]
    qr = q * cos + _swap_halves(q) * sin
    kr = (k * cos + _swap_halves(k) * sin) * (RET_QK_DIM ** -0.5)
    qb = qr.astype(BF16)
    kbt = kr.T.astype(BF16)
    inner = []
    for hh in range(RET_HEADS):
        qm = (qr * hm_ref[hh:hh + 1, :]).astype(BF16)
        sc = jnp.dot(qm, kbt, preferred_element_type=F32)
        sc = sc * intra_ref[hh]
        inner.append(jnp.dot(sc.astype(BF16), vb[:, hh * RET_V_DIM:(hh + 1) * RET_V_DIM],
                             preferred_element_type=F32))
    st = state_ref[...]
    cross = jnp.dot(qb, st.astype(BF16), preferred_element_type=F32) * qd_ref[...]
    o = jnp.concatenate(inner, axis=1) + cross
    kv = lax.dot_general((kr * kd_ref[...]).astype(BF16), vb, (((0,), (0,)), ((), ())),
                         preferred_element_type=F32)
    state_ref[...] = st * cd_ref[...] + kv * bd_ref[...]

    normed = []
    for hh in range(RET_HEADS):
        oh = o[:, hh * RET_V_DIM:(hh + 1) * RET_V_DIM]
        d = oh - jnp.mean(oh, axis=-1, keepdims=True)
        normed.append(d * lax.rsqrt(jnp.mean(d * d, axis=-1, keepdims=True) + LN_EPS))
    on = jnp.concatenate(normed, axis=1) * gn_ref[...]

    c1 = c0 + RET_WIDTH + 2 * CONV_CH
    p2 = jnp.dot(h, win_ref[:, c0:c1], preferred_element_type=F32)
    g_ret = p2[:, 0:RET_WIDTH]
    yret = (g_ret * jax.nn.sigmoid(g_ret) * on).astype(BF16)
    a_out = jnp.dot(yret, wret_ref[...], preferred_element_type=F32)

    u_val = p2[:, RET_WIDTH:RET_WIDTH + CONV_CH]
    u_gate = p2[:, RET_WIDTH + CONV_CH:RET_WIDTH + 2 * CONV_CH]
    cbuf_ref[CONV_HIST:CONV_HIST + ts, :] = u_val * jax.nn.sigmoid(u_gate)
    base = CONV_HIST - (CONV_K - 1)
    groups = [[o for o in range(base, base + CONV_K) if o % SUBLANES == r] for r in range(SUBLANES)]
    for r in range(1, SUBLANES):
        offs = groups[r]
        n_rows = ts + offs[-1] - offs[0]
        shift_ref[r, 0:n_rows, :] = cbuf_ref[pl.ds(offs[0], n_rows), :]
    for rc in range(ts // CONV_ROWS):
        acc = jnp.broadcast_to(dwb_ref[...], (CONV_ROWS, CONV_CH))
        for r in range(SUBLANES):
            for o in groups[r]:
                if r == 0:
                    tap = cbuf_ref[pl.ds(rc * CONV_ROWS + o, CONV_ROWS), :]
                else:
                    tap = shift_ref[r, pl.ds(rc * CONV_ROWS + o - groups[r][0], CONV_ROWS), :]
                acc = acc + dww_ref[o - base:o - base + 1, :] * tap
        conv_ref[rc * CONV_ROWS:(rc + 1) * CONV_ROWS, :] = acc
    cbuf_ref[0:CONV_HIST, :] = cbuf_ref[ts:ts + CONV_HIST, :]
    cv = conv_ref[...]
    d = cv - jnp.mean(cv, axis=-1, keepdims=True)
    ln = d * lax.rsqrt(jnp.mean(d * d, axis=-1, keepdims=True) + LN_EPS) * lng_ref[...] + lnb_ref[...]
    yconv = (ln * jax.nn.sigmoid(ln)).astype(BF16)
    b_out = jnp.dot(yconv, wconv_ref[...], preferred_element_type=F32) + bconv_ref[...]

    p3 = jnp.dot(h, win_ref[:, c1:], preferred_element_type=F32)
    dm = x.shape[1]
    mixed = jax.nn.sigmoid(p3[:, 0:dm]) * a_out + jax.nn.sigmoid(p3[:, dm:2 * dm]) * b_out
    o_ref[...] = x + jnp.dot(mixed.astype(BF16), wout_ref[...], preferred_element_type=F32)


def _mixer_tables(seq, ts):
    half = RET_QK_DIM // 2
    inv = 1.0 / (ROPE_BASE ** (np.arange(half, dtype=np.float64) / half))
    ang = np.arange(seq, dtype=np.float64)[:, None] * inv[None, :]
    cos = np.tile(np.concatenate([np.cos(ang), np.cos(ang)], axis=1), (1, RET_HEADS))
    sin = np.tile(np.concatenate([-np.sin(ang), np.sin(ang)], axis=1), (1, RET_HEADS))
    log_gamma = np.log1p(-(2.0 ** (-5.0 - np.arange(RET_HEADS, dtype=np.float64))))
    n = np.arange(ts, dtype=np.float64)
    diff = n[:, None] - n[None, :]
    intra = np.where(diff[None] >= 0, np.exp(log_gamma[:, None, None] * np.maximum(diff, 0.0)[None]), 0.0)
    q_decay = np.exp(log_gamma[:, None] * (n + 1.0)[None])
    k_decay = np.exp(log_gamma[:, None] * (ts - 1.0 - n)[None])
    chunk_decay = np.exp(log_gamma * ts)
    qd = np.repeat(q_decay.T, RET_V_DIM, axis=1)
    kd = np.repeat(k_decay.T, RET_QK_DIM, axis=1)
    cd = np.repeat(chunk_decay, RET_V_DIM)[None, :]
    row_head = np.arange(RET_QK) // RET_QK_DIM
    col_head = np.arange(RET_WIDTH) // RET_V_DIM
    bd = (row_head[:, None] == col_head[None, :]).astype(np.float64)
    hm = (np.arange(RET_HEADS)[:, None] == row_head[None, :]).astype(np.float64)
    as32 = lambda a: jnp.asarray(a, dtype=F32)
    return tuple(as32(a) for a in (cos, sin, qd, kd, intra, cd, bd, hm))


def _mixer_tile(seq):
    for ts in (256, 128, 64):
        if seq % ts == 0:
            return ts
    raise ValueError(f"sequence length {seq} must be a multiple of {CONV_ROWS}")


def _mixer(x, g, w_in, gn_g, dw_w, dw_b, ln_g, ln_b, w_ret, w_conv, b_conv, w_out):
    bsz, seq, dm = x.shape
    ts = _mixer_tile(seq)
    cos, sin, qd, kd, intra, cd, bd, hm = _mixer_tables(seq, ts)
    row = lambda a: a.reshape(1, -1).astype(F32)
    in_cols = w_in.shape[1]
    grid = (bsz, seq // ts)
    in_specs = [
        pl.BlockSpec((None, ts, dm), lambda b, j: (b, j, 0)),
        _const_spec((1, dm)),
        _const_spec((dm, in_cols)),
        pl.BlockSpec((ts, RET_QK), lambda b, j: (j, 0)),
        pl.BlockSpec((ts, RET_QK), lambda b, j: (j, 0)),
        _const_spec((ts, RET_WIDTH)),
        _const_spec((ts, RET_QK)),
        _const_spec((RET_HEADS, ts, ts)),
        _const_spec((1, RET_WIDTH)),
        _const_spec((RET_QK, RET_WIDTH)),
        _const_spec((RET_HEADS, RET_QK)),
        _const_spec((1, RET_WIDTH)),
        _const_spec((CONV_K, CONV_CH)),
        _const_spec((1, CONV_CH)),
        _const_spec((1, CONV_CH)),
        _const_spec((1, CONV_CH)),
        _const_spec((RET_WIDTH, dm)),
        _const_spec((CONV_CH, dm)),
        _const_spec((1, dm)),
        _const_spec((dm, dm)),
    ]
    return pl.pallas_call(
        functools.partial(_mixer_kernel, ts=ts),
        name="mixer",
        grid=grid,
        in_specs=in_specs,
        out_specs=pl.BlockSpec((None, ts, dm), lambda b, j: (b, j, 0)),
        out_shape=jax.ShapeDtypeStruct((bsz, seq, dm), F32),
        scratch_shapes=[
            pltpu.VMEM((RET_QK, RET_WIDTH), F32),
            pltpu.VMEM((CONV_HIST + ts, CONV_CH), F32),
            pltpu.VMEM((ts, CONV_CH), F32),
            pltpu.VMEM((SUBLANES, ts + CONV_HIST - SUBLANES, CONV_CH), F32),
        ],
        compiler_params=pltpu.CompilerParams(
            dimension_semantics=("arbitrary", "arbitrary"), vmem_limit_bytes=VMEM_LIMIT_BYTES),
    )(x, row(g), w_in.astype(BF16), cos, sin, qd, kd, intra, cd, bd, hm, row(gn_g), dw_w.astype(F32),
      row(dw_b), row(ln_g), row(ln_b), w_ret.astype(BF16), w_conv.astype(BF16), row(b_conv),
      w_out.astype(BF16))


def _zip_steps(*steppers):
    results = [None] * len(steppers)
    live = dict(enumerate(steppers))
    while live:
        for idx in list(live):
            try:
                next(live[idx])
            except StopIteration as done:
                results[idx] = done.value
                del live[idx]
        if live:
            yield
    return results


def _interleave(*steppers):
    zipped = _zip_steps(*steppers)
    try:
        while True:
            next(zipped)
    except StopIteration as done:
        return done.value


def _top16(s, row_id, break_ties, want_rank=True):
    cur = s
    rank = jnp.full(s.shape, NOT_SELECTED, F32) if want_rank else None
    vals = []
    for r in range(PEER_TOPK):
        m = jnp.max(cur, axis=0, keepdims=True)
        if break_ties:
            first = jnp.min(jnp.where(cur == m, row_id, 1e9), axis=0, keepdims=True)
            hit = row_id == first
        else:
            hit = cur == m
        if want_rank:
            rank = jnp.where(hit, float(r), rank)
        cur = jnp.where(hit, -jnp.inf, cur)
        vals.append(m)
        yield
    return jnp.concatenate(vals, axis=0), rank


def _staircase(v1, v2, e1x, e2x, break_ties):
    k = PEER_TOPK
    row = lax.broadcasted_iota(jnp.int32, (SUBLANES, LANES), 0).astype(F32)
    ninf = jnp.full((SUBLANES, LANES), -jnp.inf, F32)
    cand, pos, wgt = [], [], []
    for b0 in (0, SUBLANES):
        cand.append(v1[0:1] + v2[b0:b0 + SUBLANES]); pos.append(row + float(b0))
        wgt.append(e1x[0:1] * e2x[b0:b0 + SUBLANES])
    cand.append(v1[1:2] + v2[0:SUBLANES]); pos.append(row + float(k)); wgt.append(e1x[1:2] * e2x[0:SUBLANES])
    cand.append(v1[SUBLANES:k] + v2[0:1]); pos.append((row + float(SUBLANES)) * float(k))
    wgt.append(e1x[SUBLANES:k] * e2x[0:1])
    n_col = 5
    for b in range(n_col):
        a_max = float(k // (b + 1) - 1)
        ok = (row >= 2.0) & (row <= a_max)
        cand.append(jnp.where(ok, v1[0:SUBLANES] + v2[b:b + 1], ninf))
        pos.append(jnp.where(ok, row * float(k) + float(b), -1.0))
        wgt.append(e1x[0:SUBLANES] * e2x[b:b + 1])
    ng = len(cand)
    sel = [jnp.zeros((SUBLANES, LANES), F32) for _ in range(ng)]
    for _ in range(k):
        m = cand[0]
        for g in range(1, ng):
            m = jnp.maximum(m, cand[g])
        m = jnp.max(m, axis=0, keepdims=True)
        if break_ties:
            fp = jnp.where(cand[0] == m, pos[0], 1e9)
            for g in range(1, ng):
                fp = jnp.minimum(fp, jnp.where(cand[g] == m, pos[g], 1e9))
            fp = jnp.min(fp, axis=0, keepdims=True)
        for g in range(ng):
            hit = (pos[g] == fp) if break_ties else (cand[g] == m)
            sel[g] = jnp.where(hit, 1.0, sel[g])
            cand[g] = jnp.where(hit, -jnp.inf, cand[g])
        yield
    b0cnt = jnp.sum(sel[0] + sel[1], axis=0, keepdims=True)
    b1cnt = jnp.sum(sel[2], axis=0, keepdims=True)
    low = sel[4]
    for g in range(5, ng):
        low = low + sel[g]
    low = low + jnp.where(row == 0.0, b0cnt, 0.0) + jnp.where(row == 1.0, b1cnt, 0.0)
    counts = jnp.concatenate([low, sel[3]], axis=0)
    zt = sel[0] * wgt[0]
    nsel = sel[0]
    for g in range(1, ng):
        zt = zt + sel[g] * wgt[g]
        nsel = nsel + sel[g]
    return counts, jnp.sum(zt, axis=0, keepdims=True), jnp.sum(nsel, axis=0, keepdims=True)


def _peer_kernel(x_ref, g_ref, wq_ref, keys_ref, u0_ref, u_ref, vt_ref, gfin_ref, o_ref,
                 hn_ref, s_ref, vbuf_ref, w2_ref, thr_ref, w1_ref, st_ref, at_ref, acc_ref,
                 *, tt, eb, final_norm):
    j = pl.program_id(1)
    n_blk = pl.num_programs(1) - 1
    nch = tt // PEER_CHUNK
    lpc = PEER_CHUNK // LANES
    qd = 2 * PEER_HALF

    @pl.when(j == 0)
    def _routing():
        x = x_ref[...]
        hn = x * lax.rsqrt(jnp.mean(x * x, axis=-1, keepdims=True) + RMS_EPS) * g_ref[...]
        for c in range(nch):
            hn_c = hn[c * PEER_CHUNK:(c + 1) * PEER_CHUNK, :].T.astype(BF16)
            hn_ref[c] = hn_c
            acc_ref[c] = jnp.zeros(acc_ref.shape[1:], F32)
            at_ref[1, c] = jnp.zeros(at_ref.shape[2:], BF16)
            st_ref[0, c] = jnp.dot(u0_ref[...], hn_c, preferred_element_type=F32)
            for hh in range(PEER_HEADS):
                qt = jnp.dot(wq_ref[hh * qd:(hh + 1) * qd, :], hn_c, preferred_element_type=F32).astype(BF16)
                for p in range(2):
                    sc = jnp.dot(keys_ref[2 * hh + p], qt[p * PEER_HALF:(p + 1) * PEER_HALF, :],
                                 preferred_element_type=F32)
                    for l in range(lpc):
                        s_ref[2 * hh + p, c * lpc + l] = sc[:, l * LANES:(l + 1) * LANES]

        row_id = lax.broadcasted_iota(jnp.int32, (N_KEYS, LANES), 0).astype(F32)

        n_route = PEER_HEADS * nch * lpc
        topk = float(PEER_TOPK)

        def scores(blk):
            hh = blk // (nch * lpc)
            lb = blk % (nch * lpc)
            return hh, lb, s_ref[2 * hh, lb], s_ref[2 * hh + 1, lb]

        def dense_out(hh, lb, s1, s2, v1, v2, z, thr):
            thr_ref[hh, lb] = thr
            w1_ref[hh, lb] = jnp.exp(s1 - v1[0:1]) * (0.5 / z)
            w2_ref[hh, lb] = jnp.exp(s2 - v2[0:1])

        def fast_a(blk, slot):
            hh, lb, s1, s2 = scores(blk)
            (v1, _), (v2, _) = yield from _zip_steps(_top16(s1, row_id, False, want_rank=False),
                                                     _top16(s2, row_id, False, want_rank=False))
            last = slice(PEER_TOPK - 1, PEER_TOPK)
            in_top = jnp.where(s1 >= v1[last], 1.0, 0.0) + jnp.where(s2 >= v2[last], 1.0, 0.0)
            vbuf_ref[slot, 0] = v1
            vbuf_ref[slot, 1] = v2
            vbuf_ref[slot, 2, 0:1, :] = jnp.sum(in_top, axis=0, keepdims=True)

        def fast_b(blk, slot):
            hh, lb, s1, s2 = scores(blk)
            v1 = vbuf_ref[slot, 0]
            v2 = vbuf_ref[slot, 1]
            counts, z, nsel = yield from _staircase(v1, v2, jnp.exp(v1 - v1[0:1]), jnp.exp(v2 - v2[0:1]), False)
            tval = jnp.full((PEER_TOPK, LANES), jnp.inf, F32)
            for b in range(PEER_TOPK):
                tval = jnp.where(counts == float(b + 1), v2[b:b + 1], tval)
            thr = jnp.full((N_KEYS, LANES), jnp.inf, F32)
            for a in range(SUBLANES):
                thr = jnp.where(s1 == v1[a:a + 1], tval[a:a + 1], thr)
            lo = jnp.min(jnp.where(counts[SUBLANES:] > 0.0, v1[SUBLANES:], jnp.inf), axis=0, keepdims=True)
            thr = jnp.where(s1 < v1[SUBLANES - 1:SUBLANES], jnp.where(s1 >= lo, v2[0:1], thr), thr)
            dense_out(hh, lb, s1, s2, v1, v2, z, thr)
            return vbuf_ref[slot, 2, 0:1, :] + nsel

        def exact(blk):
            hh, lb, s1, s2 = scores(blk)
            (v1, rank1), (v2, rank2) = _interleave(_top16(s1, row_id, True), _top16(s2, row_id, True))
            (counts, z, _), = _interleave(
                _staircase(v1, v2, jnp.exp(v1 - v1[0:1]), jnp.exp(v2 - v2[0:1]), True))
            bcnt = jnp.zeros((N_KEYS, LANES), F32)
            for a in range(PEER_TOPK):
                bcnt = jnp.where(rank1 == float(a), counts[a:a + 1], bcnt)
            dense_out(hh, lb, s1, s2, v1, v2, z, jnp.where(bcnt > 0.0, 1.0 - bcnt, jnp.inf))
            s_ref[2 * hh + 1, lb] = -rank2

        _interleave(fast_a(0, 0))

        def route(i, carry):
            slot = i % 2
            picked, _ = _interleave(fast_b(i, slot), fast_a(jnp.minimum(i + 1, n_route - 1), 1 - slot))
            tied = jnp.max(jnp.where(picked == 3.0 * topk, 0.0, 1.0))

            @pl.when(tied > 0.0)
            def _():
                exact(i)

            return carry

        lax.fori_loop(0, n_route, route, 0)

    def steady(cur, nxt):
        def chunk(c, carry):
            acc_ref[c] += jnp.dot(vt_ref[...], at_ref[nxt, c], preferred_element_type=F32)
            for e in range(eb):
                e1 = j * eb + e
                for l in range(lpc):
                    lb = c * lpc + l
                    cols = slice(l * LANES, (l + 1) * LANES)
                    thr = [thr_ref[hh, lb, pl.ds(e1, 1), :] for hh in range(PEER_HEADS)]
                    w1 = [w1_ref[hh, lb, pl.ds(e1, 1), :] for hh in range(PEER_HEADS)]
                    for g in range(N_KEYS // BUILD_ROWS):
                        grow = slice(g * BUILD_ROWS, (g + 1) * BUILD_ROWS)
                        rows = slice(e * N_KEYS + g * BUILD_ROWS, e * N_KEYS + (g + 1) * BUILD_ROWS)
                        xs = st_ref[cur, c, rows, cols]
                        ge = xs * (1.0 + lax.erf(xs * INV_SQRT2))
                        gate = jnp.zeros((BUILD_ROWS, LANES), F32)
                        for hh in range(PEER_HEADS):
                            gate = gate + jnp.where(s_ref[2 * hh + 1, lb, grow, :] >= thr[hh],
                                                    w2_ref[hh, lb, grow, :], 0.0) * w1[hh]
                        at_ref[cur, c, rows, cols] = (ge * gate).astype(BF16)
            st_ref[nxt, c] = jnp.dot(u_ref[...], hn_ref[c], preferred_element_type=F32)
            return carry

        lax.fori_loop(0, nch, chunk, 0)

    @pl.when((j < n_blk) & (j % 2 == 0))
    def _even():
        steady(0, 1)

    @pl.when((j < n_blk) & (j % 2 == 1))
    def _odd():
        steady(1, 0)

    @pl.when(j == n_blk)
    def _finish():
        last = (N_KEYS // eb - 1) % 2
        for c in range(nch):
            acc = acc_ref[c] + jnp.dot(vt_ref[...], at_ref[last, c], preferred_element_type=F32)
            rows = slice(c * PEER_CHUNK, (c + 1) * PEER_CHUNK)
            y = acc.T + x_ref[rows, :]
            if final_norm:
                y = y * lax.rsqrt(jnp.mean(y * y, axis=-1, keepdims=True) + RMS_EPS) * gfin_ref[...]
            o_ref[rows, :] = y


def _peer_tiles(tokens):
    for tt in (512, 256):
        if tokens % tt == 0:
            return tt, 8
    raise ValueError(f"token count {tokens} must be a multiple of {PEER_CHUNK}")


def _peer(x, g, w_query, sub_keys, u_tab, v_tab, g_final, final_norm):
    tokens, dm = x.shape
    tt, eb = _peer_tiles(tokens)
    nlb = tt // LANES
    nch = tt // PEER_CHUNK
    eblk = eb * N_KEYS
    n_exp = u_tab.shape[0]
    assert n_exp == N_KEYS * N_KEYS and N_KEYS % eb == 0
    assert sub_keys.shape == (PEER_HEADS, 2, N_KEYS, PEER_HALF)
    n_blk = N_KEYS // eb
    wq_t = w_query.T.astype(BF16)
    keys = sub_keys.reshape(PEER_HEADS * 2, N_KEYS, PEER_HALF).astype(BF16)
    u_bf = u_tab.astype(BF16)
    vt_bf = v_tab.T.astype(BF16)
    row = lambda a: a.reshape(1, -1).astype(F32)
    grid = (tokens // tt, n_blk + 1)
    return pl.pallas_call(
        functools.partial(_peer_kernel, tt=tt, eb=eb, final_norm=final_norm),
        name="peer",
        grid=grid,
        in_specs=[
            pl.BlockSpec((tt, dm), lambda i, j: (i, 0)),
            _const_spec((1, dm)),
            _const_spec(wq_t.shape),
            _const_spec(keys.shape),
            _const_spec((eblk, dm)),
            pl.BlockSpec((eblk, dm), lambda i, j: (jnp.minimum(j + 1, n_blk - 1), 0)),
            pl.BlockSpec((dm, eblk), lambda i, j: (0, jnp.clip(j - 1, 0, n_blk - 1))),
            _const_spec((1, dm)),
        ],
        out_specs=pl.BlockSpec((tt, dm), lambda i, j: (i, 0)),
        out_shape=jax.ShapeDtypeStruct((tokens, dm), F32),
        scratch_shapes=[
            pltpu.VMEM((nch, dm, PEER_CHUNK), BF16),
            pltpu.VMEM((2 * PEER_HEADS, nlb, N_KEYS, LANES), F32),
            pltpu.VMEM((2, 3, PEER_TOPK, LANES), F32),
            pltpu.VMEM((PEER_HEADS, nlb, N_KEYS, LANES), F32),
            pltpu.VMEM((PEER_HEADS, nlb, N_KEYS, LANES), F32),
            pltpu.VMEM((PEER_HEADS, nlb, N_KEYS, LANES), F32),
            pltpu.VMEM((2, nch, eblk, PEER_CHUNK), F32),
            pltpu.VMEM((2, nch, eblk, PEER_CHUNK), BF16),
            pltpu.VMEM((nch, dm, PEER_CHUNK), F32),
        ],
        compiler_params=pltpu.CompilerParams(
            dimension_semantics=("arbitrary", "arbitrary"), vmem_limit_bytes=VMEM_LIMIT_BYTES),
    )(x, row(g), wq_t, keys, u_bf, u_bf, vt_bf, row(g_final))


def kernel(x, rms_mix_g, w_in, ret_gn_g, conv_dw_w, conv_dw_b, conv_ln_g, conv_ln_b, w_ret_proj, w_conv_proj,
           b_conv_proj, w_out, rms_ffn_g, w_query, peer_sub_keys, peer_u, peer_v, rms_final_g):
    bsz, seq, dm = x.shape
    depth = w_in.shape[0]
    for l in range(depth):
        x = _mixer(x, rms_mix_g[l], w_in[l], ret_gn_g[l], conv_dw_w[l], conv_dw_b[l], conv_ln_g[l],
                   conv_ln_b[l], w_ret_proj[l], w_conv_proj[l], b_conv_proj[l], w_out[l])
        x = _peer(x.reshape(bsz * seq, dm), rms_ffn_g[l], w_query[l], peer_sub_keys[l], peer_u[l], peer_v[l],
                  rms_final_g, final_norm=(l == depth - 1)).reshape(bsz, seq, dm)
    return x
```

```python
import functools

import numpy as np
import jax
import jax.numpy as jnp
from jax import lax
from jax.experimental import pallas as pl
from jax.experimental.pallas import tpu as pltpu

F32 = jnp.float32
BF16 = jnp.bfloat16

RET_HEADS = 4
RET_QK_DIM = 64
RET_V_DIM = 128
RET_QK = RET_HEADS * RET_QK_DIM
RET_WIDTH = RET_HEADS * RET_V_DIM
ROPE_BASE = 10000.0
CONV_CH = 512
CONV_K = 31
PEER_HEADS = 8
PEER_HALF = 128
N_KEYS = 128
PEER_TOPK = 16
RMS_EPS = 1e-6
LN_EPS = 1e-5
INV_SQRT2 = 0.7071067811865476

LANES = 128
SUBLANES = 8
VMEM_LIMIT_BYTES = 56 * 1024 * 1024
CONV_HIST = 32
CONV_ROWS = 32
PEER_CHUNK = 256
BUILD_ROWS = 32
NOT_SELECTED = 127.0


def _const_spec(shape):
    nd = len(shape)
    return pl.BlockSpec(shape, lambda *_: (0,) * nd, pipeline_mode=pl.Buffered(1))


def _swap_halves(t):
    half = RET_QK_DIM // 2
    parts = []
    for c in range(t.shape[1] // LANES):
        tc = t[:, c * LANES:(c + 1) * LANES]
        lane = lax.broadcasted_iota(jnp.int32, tc.shape, 1)
        first = (lane % RET_QK_DIM) < half
        nxt = pltpu.roll(tc, LANES - half, 1)
        prv = pltpu.roll(tc, half, 1)
        parts.append(jnp.where(first, nxt, prv))
    return jnp.concatenate(parts, axis=1)


def _mixer_kernel(x_ref, g_ref, win_ref, cos_ref, sin_ref, qd_ref, kd_ref, intra_ref, cd_ref, bd_ref,
                  hm_ref, gn_ref, dww_ref, dwb_ref, lng_ref, lnb_ref, wret_ref, wconv_ref, bconv_ref,
                  wout_ref, o_ref, state_ref, cbuf_ref, conv_ref, shift_ref, *, ts):
    @pl.when(pl.program_id(1) == 0)
    def _():
        state_ref[...] = jnp.zeros_like(state_ref)
        cbuf_ref[0:CONV_HIST, :] = jnp.zeros((CONV_HIST, CONV_CH), F32)

    x = x_ref[...]
    h = (x * lax.rsqrt(jnp.mean(x * x, axis=-1, keepdims=True) + RMS_EPS) * g_ref[...]).astype(BF16)

    c0 = 2 * RET_QK + RET_WIDTH
    qkv = jnp.dot(h, win_ref[:, 0:c0], preferred_element_type=F32)
    q = qkv[:, 0:RET_QK]
    k = qkv[:, RET_QK:2 * RET_QK]
    vb = qkv[:, 2 * RET_QK:c0].astype(BF16)
    cos = cos_ref[...]
    sin = sin_ref[...]
    qr = q * cos + _swap_halves(q) * sin
    kr = (k * cos + _swap_halves(k) * sin) * (RET_QK_DIM ** -0.5)
    qb = qr.astype(BF16)
    kbt = kr.T.astype(BF16)
    inner = []
    for hh in range(RET_HEADS):
        qm = (qr * hm_ref[hh:hh + 1, :]).astype(BF16)
        sc = jnp.dot(qm, kbt, preferred_element_type=F32)
        sc = sc * intra_ref[hh]
        inner.append(jnp.dot(sc.astype(BF16), vb[:, hh * RET_V_DIM:(hh + 1) * RET_V_DIM],
                             preferred_element_type=F32))
    st = state_ref[...]
    cross = jnp.dot(qb, st.astype(BF16), preferred_element_type=F32) * qd_ref[...]
    o = jnp.concatenate(inner, axis=1) + cross
    kv = lax.dot_general((kr * kd_ref[...]).astype(BF16), vb, (((0,), (0,)), ((), ())),
                         preferred_element_type=F32)
    state_ref[...] = st * cd_ref[...] + kv * bd_ref[...]

    normed = []
    for hh in range(RET_HEADS):
        oh = o[:, hh * RET_V_DIM:(hh + 1) * RET_V_DIM]
        d = oh - jnp.mean(oh, axis=-1, keepdims=True)
        normed.append(d * lax.rsqrt(jnp.mean(d * d, axis=-1, keepdims=True) + LN_EPS))
    on = jnp.concatenate(normed, axis=1) * gn_ref[...]

    c1 = c0 + RET_WIDTH + 2 * CONV_CH
    p2 = jnp.dot(h, win_ref[:, c0:c1], preferred_element_type=F32)
    g_ret = p2[:, 0:RET_WIDTH]
    yret = (g_ret * jax.nn.sigmoid(g_ret) * on).astype(BF16)
    a_out = jnp.dot(yret, wret_ref[...], preferred_element_type=F32)

    u_val = p2[:, RET_WIDTH:RET_WIDTH + CONV_CH]
    u_gate = p2[:, RET_WIDTH + CONV_CH:RET_WIDTH + 2 * CONV_CH]
    cbuf_ref[CONV_HIST:CONV_HIST + ts, :] = u_val * jax.nn.sigmoid(u_gate)
    base = CONV_HIST - (CONV_K - 1)
    groups = [[o for o in range(base, base + CONV_K) if o % SUBLANES == r] for r in range(SUBLANES)]
    for r in range(1, SUBLANES):
        offs = groups[r]
        n_rows = ts + offs[-1] - offs[0]
        shift_ref[r, 0:n_rows, :] = cbuf_ref[pl.ds(offs[0], n_rows), :]
    for rc in range(ts // CONV_ROWS):
        acc = jnp.broadcast_to(dwb_ref[...], (CONV_ROWS, CONV_CH))
        for r in range(SUBLANES):
            for o in groups[r]:
                if r == 0:
                    tap = cbuf_ref[pl.ds(rc * CONV_ROWS + o, CONV_ROWS), :]
                else:
                    tap = shift_ref[r, pl.ds(rc * CONV_ROWS + o - groups[r][0], CONV_ROWS), :]
                acc = acc + dww_ref[o - base:o - base + 1, :] * tap
        conv_ref[rc * CONV_ROWS:(rc + 1) * CONV_ROWS, :] = acc
    cbuf_ref[0:CONV_HIST, :] = cbuf_ref[ts:ts + CONV_HIST, :]
    cv = conv_ref[...]
    d = cv - jnp.mean(cv, axis=-1, keepdims=True)
    ln = d * lax.rsqrt(jnp.mean(d * d, axis=-1, keepdims=True) + LN_EPS) * lng_ref[...] + lnb_ref[...]
    yconv = (ln * jax.nn.sigmoid(ln)).astype(BF16)
    b_out = jnp.dot(yconv, wconv_ref[...], preferred_element_type=F32) + bconv_ref[...]

    p3 = jnp.dot(h, win_ref[:, c1:], preferred_element_type=F32)
    dm = x.shape[1]
    mixed = jax.nn.sigmoid(p3[:, 0:dm]) * a_out + jax.nn.sigmoid(p3[:, dm:2 * dm]) * b_out
    o_ref[...] = x + jnp.dot(mixed.astype(BF16), wout_ref[...], preferred_element_type=F32)


def _mixer_tables(seq, ts):
    half = RET_QK_DIM // 2
    inv = 1.0 / (ROPE_BASE ** (np.arange(half, dtype=np.float64) / half))
    ang = np.arange(seq, dtype=np.float64)[:, None] * inv[None, :]
    cos = np.tile(np.concatenate([np.cos(ang), np.cos(ang)], axis=1), (1, RET_HEADS))
    sin = np.tile(np.concatenate([-np.sin(ang), np.sin(ang)], axis=1), (1, RET_HEADS))
    log_gamma = np.log1p(-(2.0 ** (-5.0 - np.arange(RET_HEADS, dtype=np.float64))))
    n = np.arange(ts, dtype=np.float64)
    diff = n[:, None] - n[None, :]
    intra = np.where(diff[None] >= 0, np.exp(log_gamma[:, None, None] * np.maximum(diff, 0.0)[None]), 0.0)
    q_decay = np.exp(log_gamma[:, None] * (n + 1.0)[None])
    k_decay = np.exp(log_gamma[:, None] * (ts - 1.0 - n)[None])
    chunk_decay = np.exp(log_gamma * ts)
    qd = np.repeat(q_decay.T, RET_V_DIM, axis=1)
    kd = np.repeat(k_decay.T, RET_QK_DIM, axis=1)
    cd = np.repeat(chunk_decay, RET_V_DIM)[None, :]
    row_head = np.arange(RET_QK) // RET_QK_DIM
    col_head = np.arange(RET_WIDTH) // RET_V_DIM
    bd = (row_head[:, None] == col_head[None, :]).astype(np.float64)
    hm = (np.arange(RET_HEADS)[:, None] == row_head[None, :]).astype(np.float64)
    as32 = lambda a: jnp.asarray(a, dtype=F32)
    return tuple(as32(a) for a in (cos, sin, qd, kd, intra, cd, bd, hm))


def _mixer_tile(seq):
    for ts in (256, 128, 64):
        if seq % ts == 0:
            return ts
    raise ValueError(f"sequence length {seq} must be a multiple of {CONV_ROWS}")


def _mixer(x, g, w_in, gn_g, dw_w, dw_b, ln_g, ln_b, w_ret, w_conv, b_conv, w_out):
    bsz, seq, dm = x.shape
    ts = _mixer_tile(seq)
    cos, sin, qd, kd, intra, cd, bd, hm = _mixer_tables(seq, ts)
    row = lambda a: a.reshape(1, -1).astype(F32)
    in_cols = w_in.shape[1]
    grid = (bsz, seq // ts)
    in_specs = [
        pl.BlockSpec((None, ts, dm), lambda b, j: (b, j, 0)),
        _const_spec((1, dm)),
        _const_spec((dm, in_cols)),
        pl.BlockSpec((ts, RET_QK), lambda b, j: (j, 0)),
        pl.BlockSpec((ts, RET_QK), lambda b, j: (j, 0)),
        _const_spec((ts, RET_WIDTH)),
        _const_spec((ts, RET_QK)),
        _const_spec((RET_HEADS, ts, ts)),
        _const_spec((1, RET_WIDTH)),
        _const_spec((RET_QK, RET_WIDTH)),
        _const_spec((RET_HEADS, RET_QK)),
        _const_spec((1, RET_WIDTH)),
        _const_spec((CONV_K, CONV_CH)),
        _const_spec((1, CONV_CH)),
        _const_spec((1, CONV_CH)),
        _const_spec((1, CONV_CH)),
        _const_spec((RET_WIDTH, dm)),
        _const_spec((CONV_CH, dm)),
        _const_spec((1, dm)),
        _const_spec((dm, dm)),
    ]
    return pl.pallas_call(
        functools.partial(_mixer_kernel, ts=ts),
        name="mixer",
        grid=grid,
        in_specs=in_specs,
        out_specs=pl.BlockSpec((None, ts, dm), lambda b, j: (b, j, 0)),
        out_shape=jax.ShapeDtypeStruct((bsz, seq, dm), F32),
        scratch_shapes=[
            pltpu.VMEM((RET_QK, RET_WIDTH), F32),
            pltpu.VMEM((CONV_HIST + ts, CONV_CH), F32),
            pltpu.VMEM((ts, CONV_CH), F32),
            pltpu.VMEM((SUBLANES, ts + CONV_HIST - SUBLANES, CONV_CH), F32),
        ],
        compiler_params=pltpu.CompilerParams(
            dimension_semantics=("arbitrary", "arbitrary"), vmem_limit_bytes=VMEM_LIMIT_BYTES),
    )(x, row(g), w_in.astype(BF16), cos, sin, qd, kd, intra, cd, bd, hm, row(gn_g), dw_w.astype(F32),
      row(dw_b), row(ln_g), row(ln_b), w_ret.astype(BF16), w_conv.astype(BF16), row(b_conv),
      w_out.astype(BF16))


def _zip_steps(*steppers):
    results = [None] * len(steppers)
    live = dict(enumerate(steppers))
    while live:
        for idx in list(live):
            try:
                next(live[idx])
            except StopIteration as done:
                results[idx] = done.value
                del live[idx]
        if live:
            yield
    return results


def _interleave(*steppers):
    zipped = _zip_steps(*steppers)
    try:
        while True:
            next(zipped)
    except StopIteration as done:
        return done.value


def _top16(s, row_id, break_ties, want_rank=True):
    cur = s
    rank = jnp.full(s.shape, NOT_SELECTED, F32) if want_rank else None
    vals = []
    for r in range(PEER_TOPK):
        m = jnp.max(cur, axis=0, keepdims=True)
        if break_ties:
            first = jnp.min(jnp.where(cur == m, row_id, 1e9), axis=0, keepdims=True)
            hit = row_id == first
        else:
            hit = cur == m
        if want_rank:
            rank = jnp.where(hit, float(r), rank)
        cur = jnp.where(hit, -jnp.inf, cur)
        vals.append(m)
        yield
    return jnp.concatenate(vals, axis=0), rank


def _merge_exchange_network(n):
    pairs = []
    t = n.bit_length() - 1
    p = 1 << (t - 1)
    while p > 0:
        q, r, d = 1 << (t - 1), 0, p
        while d > 0:
            pairs += [(i, i + d) for i in range(n - d) if (i & p) == r]
            d, q, r = q - p, q >> 1, p
        p >>= 1
    return pairs


def _top16_values(rows):
    k = PEER_TOPK
    n_real = len(rows)
    assert n_real <= k
    x = list(rows) + [None] * (k - n_real)
    for n, (i, j) in enumerate(_merge_exchange_network(k)):
        if x[i] is None:
            x[i], x[j] = x[j], None
        elif x[j] is not None:
            x[i], x[j] = jnp.maximum(x[i], x[j]), jnp.minimum(x[i], x[j])
        if n % 8 == 7:
            yield
    x = [jnp.full((SUBLANES, LANES), -jnp.inf, F32) if xi is None else xi for xi in x]
    shift = SUBLANES // 2
    while shift >= 1:
        x = [jnp.maximum(x[i], pltpu.roll(x[k - 1 - i], shift, 0)) for i in range(k)]
        yield
        stride = k // 2
        while stride >= 1:
            for i in range(k):
                if (i & stride) == 0:
                    x[i], x[i + stride] = jnp.maximum(x[i], x[i + stride]), jnp.minimum(x[i], x[i + stride])
            stride //= 2
            yield
        shift //= 2
    dup = jnp.zeros((SUBLANES, LANES), F32)
    for i in range(k - 1):
        dup = jnp.where(x[i] == x[i + 1], 1.0, dup)
    row = lax.broadcasted_iota(jnp.int32, (SUBLANES, LANES), 0)
    halves = []
    for h0 in (0, SUBLANES):
        v = x[h0]
        for i in range(1, SUBLANES):
            v = jnp.where(row == i, x[h0 + i], v)
        halves.append(v)
    return jnp.concatenate(halves, axis=0), dup


def _staircase(v1, v2, e1x, e2x, break_ties):
    k = PEER_TOPK
    row = lax.broadcasted_iota(jnp.int32, (SUBLANES, LANES), 0).astype(F32)
    ninf = jnp.full((SUBLANES, LANES), -jnp.inf, F32)
    cand, pos, wgt = [], [], []
    for b0 in (0, SUBLANES):
        cand.append(v1[0:1] + v2[b0:b0 + SUBLANES]); pos.append(row + float(b0))
        wgt.append(e1x[0:1] * e2x[b0:b0 + SUBLANES])
    cand.append(v1[1:2] + v2[0:SUBLANES]); pos.append(row + float(k)); wgt.append(e1x[1:2] * e2x[0:SUBLANES])
    cand.append(v1[SUBLANES:k] + v2[0:1]); pos.append((row + float(SUBLANES)) * float(k))
    wgt.append(e1x[SUBLANES:k] * e2x[0:1])
    n_col = 5
    for b in range(n_col):
        a_max = float(k // (b + 1) - 1)
        ok = (row >= 2.0) & (row <= a_max)
        cand.append(jnp.where(ok, v1[0:SUBLANES] + v2[b:b + 1], ninf))
        pos.append(jnp.where(ok, row * float(k) + float(b), -1.0))
        wgt.append(e1x[0:SUBLANES] * e2x[b:b + 1])
    ng = len(cand)
    sel = [jnp.zeros((SUBLANES, LANES), F32) for _ in range(ng)]
    if not break_ties:
        best, _ = yield from _top16_values(cand)
        sel = [jnp.where(c >= best[k - 1:k], 1.0, 0.0) for c in cand]
    for _ in range(k if break_ties else 0):
        m = cand[0]
        for g in range(1, ng):
            m = jnp.maximum(m, cand[g])
        m = jnp.max(m, axis=0, keepdims=True)
        fp = jnp.where(cand[0] == m, pos[0], 1e9)
        for g in range(1, ng):
            fp = jnp.minimum(fp, jnp.where(cand[g] == m, pos[g], 1e9))
        fp = jnp.min(fp, axis=0, keepdims=True)
        for g in range(ng):
            hit = pos[g] == fp
            sel[g] = jnp.where(hit, 1.0, sel[g])
            cand[g] = jnp.where(hit, -jnp.inf, cand[g])
        yield
    b0cnt = jnp.sum(sel[0] + sel[1], axis=0, keepdims=True)
    b1cnt = jnp.sum(sel[2], axis=0, keepdims=True)
    low = sel[4]
    for g in range(5, ng):
        low = low + sel[g]
    low = low + jnp.where(row == 0.0, b0cnt, 0.0) + jnp.where(row == 1.0, b1cnt, 0.0)
    counts = jnp.concatenate([low, sel[3]], axis=0)
    zt = sel[0] * wgt[0]
    nsel = sel[0]
    for g in range(1, ng):
        zt = zt + sel[g] * wgt[g]
        nsel = nsel + sel[g]
    return counts, jnp.sum(zt, axis=0, keepdims=True), jnp.sum(nsel, axis=0, keepdims=True)


def _peer_kernel(x_ref, g_ref, wq_ref, keys_ref, u0_ref, u_ref, vt_ref, gfin_ref, o_ref,
                 hn_ref, s_ref, vbuf_ref, w2_ref, thr_ref, w1_ref, st_ref, at_ref, acc_ref,
                 *, tt, eb, final_norm):
    j = pl.program_id(1)
    n_blk = pl.num_programs(1) - 1
    nch = tt // PEER_CHUNK
    lpc = PEER_CHUNK // LANES
    qd = 2 * PEER_HALF

    @pl.when(j == 0)
    def _routing():
        x = x_ref[...]
        hn = x * lax.rsqrt(jnp.mean(x * x, axis=-1, keepdims=True) + RMS_EPS) * g_ref[...]
        for c in range(nch):
            hn_c = hn[c * PEER_CHUNK:(c + 1) * PEER_CHUNK, :].T.astype(BF16)
            hn_ref[c] = hn_c
            acc_ref[c] = jnp.zeros(acc_ref.shape[1:], F32)
            at_ref[1, c] = jnp.zeros(at_ref.shape[2:], BF16)
            st_ref[0, c] = jnp.dot(u0_ref[...], hn_c, preferred_element_type=F32)
            for hh in range(PEER_HEADS):
                qt = jnp.dot(wq_ref[hh * qd:(hh + 1) * qd, :], hn_c, preferred_element_type=F32).astype(BF16)
                for p in range(2):
                    sc = jnp.dot(keys_ref[2 * hh + p], qt[p * PEER_HALF:(p + 1) * PEER_HALF, :],
                                 preferred_element_type=F32)
                    for l in range(lpc):
                        s_ref[2 * hh + p, c * lpc + l] = sc[:, l * LANES:(l + 1) * LANES]

        row_id = lax.broadcasted_iota(jnp.int32, (N_KEYS, LANES), 0).astype(F32)

        n_route = PEER_HEADS * nch * lpc
        topk = float(PEER_TOPK)

        def scores(blk):
            hh = blk // (nch * lpc)
            lb = blk % (nch * lpc)
            return hh, lb, s_ref[2 * hh, lb], s_ref[2 * hh + 1, lb]

        def dense_out(hh, lb, s1, s2, v1, v2, z, thr):
            thr_ref[hh, lb] = thr
            w1_ref[hh, lb] = jnp.exp(s1 - v1[0:1]) * (0.5 / z)
            w2_ref[hh, lb] = jnp.exp(s2 - v2[0:1])

        def fast_a(blk, slot):
            hh, lb, s1, s2 = scores(blk)
            groups = lambda s: [s[i * SUBLANES:(i + 1) * SUBLANES, :] for i in range(N_KEYS // SUBLANES)]
            (v1, dup1), (v2, dup2) = yield from _zip_steps(_top16_values(groups(s1)), _top16_values(groups(s2)))
            last = slice(PEER_TOPK - 1, PEER_TOPK)
            in_top = jnp.where(s1 >= v1[last], 1.0, 0.0) + jnp.where(s2 >= v2[last], 1.0, 0.0)
            vbuf_ref[slot, 0] = v1
            vbuf_ref[slot, 1] = v2
            vbuf_ref[slot, 2, 0:1, :] = (jnp.sum(in_top, axis=0, keepdims=True)
                                         + jnp.max(dup1 + dup2, axis=0, keepdims=True))

        def fast_b(blk, slot):
            hh, lb, s1, s2 = scores(blk)
            v1 = vbuf_ref[slot, 0]
            v2 = vbuf_ref[slot, 1]
            counts, z, nsel = yield from _staircase(v1, v2, jnp.exp(v1 - v1[0:1]), jnp.exp(v2 - v2[0:1]), False)
            tval = jnp.full((PEER_TOPK, LANES), jnp.inf, F32)
            for b in range(PEER_TOPK):
                tval = jnp.where(counts == float(b + 1), v2[b:b + 1], tval)
            thr = jnp.full((N_KEYS, LANES), jnp.inf, F32)
            for a in range(SUBLANES):
                thr = jnp.where(s1 == v1[a:a + 1], tval[a:a + 1], thr)
            lo = jnp.min(jnp.where(counts[SUBLANES:] > 0.0, v1[SUBLANES:], jnp.inf), axis=0, keepdims=True)
            thr = jnp.where(s1 < v1[SUBLANES - 1:SUBLANES], jnp.where(s1 >= lo, v2[0:1], thr), thr)
            dense_out(hh, lb, s1, s2, v1, v2, z, thr)
            return vbuf_ref[slot, 2, 0:1, :] + nsel

        def exact(blk):
            hh, lb, s1, s2 = scores(blk)
            (v1, rank1), (v2, rank2) = _interleave(_top16(s1, row_id, True), _top16(s2, row_id, True))
            (counts, z, _), = _interleave(
                _staircase(v1, v2, jnp.exp(v1 - v1[0:1]), jnp.exp(v2 - v2[0:1]), True))
            bcnt = jnp.zeros((N_KEYS, LANES), F32)
            for a in range(PEER_TOPK):
                bcnt = jnp.where(rank1 == float(a), counts[a:a + 1], bcnt)
            dense_out(hh, lb, s1, s2, v1, v2, z, jnp.where(bcnt > 0.0, 1.0 - bcnt, jnp.inf))
            s_ref[2 * hh + 1, lb] = -rank2

        _interleave(fast_a(0, 0))

        def route(i, carry):
            slot = i % 2
            _, picked = _interleave(fast_a(jnp.minimum(i + 1, n_route - 1), 1 - slot), fast_b(i, slot))
            tied = jnp.max(jnp.where(picked == 3.0 * topk, 0.0, 1.0))

            @pl.when(tied > 0.0)
            def _():
                exact(i)

            return carry

        lax.fori_loop(0, n_route, route, 0)

    def steady(cur, nxt):
        def chunk(c, carry):
            acc_ref[c] += jnp.dot(vt_ref[...], at_ref[nxt, c], preferred_element_type=F32)
            for e in range(eb):
                e1 = j * eb + e
                for l in range(lpc):
                    lb = c * lpc + l
                    cols = slice(l * LANES, (l + 1) * LANES)
                    thr = [thr_ref[hh, lb, pl.ds(e1, 1), :] for hh in range(PEER_HEADS)]
                    w1 = [w1_ref[hh, lb, pl.ds(e1, 1), :] for hh in range(PEER_HEADS)]
                    for g in range(N_KEYS // BUILD_ROWS):
                        grow = slice(g * BUILD_ROWS, (g + 1) * BUILD_ROWS)
                        rows = slice(e * N_KEYS + g * BUILD_ROWS, e * N_KEYS + (g + 1) * BUILD_ROWS)
                        xs = st_ref[cur, c, rows, cols]
                        ge = xs * (1.0 + lax.erf(xs * INV_SQRT2))
                        gate = jnp.zeros((BUILD_ROWS, LANES), F32)
                        for hh in range(PEER_HEADS):
                            gate = gate + jnp.where(s_ref[2 * hh + 1, lb, grow, :] >= thr[hh],
                                                    w2_ref[hh, lb, grow, :], 0.0) * w1[hh]
                        at_ref[cur, c, rows, cols] = (ge * gate).astype(BF16)
            st_ref[nxt, c] = jnp.dot(u_ref[...], hn_ref[c], preferred_element_type=F32)
            return carry

        lax.fori_loop(0, nch, chunk, 0)

    @pl.when((j < n_blk) & (j % 2 == 0))
    def _even():
        steady(0, 1)

    @pl.when((j < n_blk) & (j % 2 == 1))
    def _odd():
        steady(1, 0)

    @pl.when(j == n_blk)
    def _finish():
        last = (N_KEYS // eb - 1) % 2
        for c in range(nch):
            acc = acc_ref[c] + jnp.dot(vt_ref[...], at_ref[last, c], preferred_element_type=F32)
            rows = slice(c * PEER_CHUNK, (c + 1) * PEER_CHUNK)
            y = acc.T + x_ref[rows, :]
            if final_norm:
                y = y * lax.rsqrt(jnp.mean(y * y, axis=-1, keepdims=True) + RMS_EPS) * gfin_ref[...]
            o_ref[rows, :] = y


def _peer_tiles(tokens):
    for tt in (512, 256):
        if tokens % tt == 0:
            return tt, 8
    raise ValueError(f"token count {tokens} must be a multiple of {PEER_CHUNK}")


def _peer(x, g, w_query, sub_keys, u_tab, v_tab, g_final, final_norm):
    tokens, dm = x.shape
    tt, eb = _peer_tiles(tokens)
    nlb = tt // LANES
    nch = tt // PEER_CHUNK
    eblk = eb * N_KEYS
    n_exp = u_tab.shape[0]
    assert n_exp == N_KEYS * N_KEYS and N_KEYS % eb == 0
    assert sub_keys.shape == (PEER_HEADS, 2, N_KEYS, PEER_HALF)
    n_blk = N_KEYS // eb
    wq_t = w_query.T.astype(BF16)
    keys = sub_keys.reshape(PEER_HEADS * 2, N_KEYS, PEER_HALF).astype(BF16)
    u_bf = u_tab.astype(BF16)
    vt_bf = v_tab.T.astype(BF16)
    row = lambda a: a.reshape(1, -1).astype(F32)
    grid = (tokens // tt, n_blk + 1)
    return pl.pallas_call(
        functools.partial(_peer_kernel, tt=tt, eb=eb, final_norm=final_norm),
        name="peer",
        grid=grid,
        in_specs=[
            pl.BlockSpec((tt, dm), lambda i, j: (i, 0)),
            _const_spec((1, dm)),
            _const_spec(wq_t.shape),
            _const_spec(keys.shape),
            _const_spec((eblk, dm)),
            pl.BlockSpec((eblk, dm), lambda i, j: (jnp.minimum(j + 1, n_blk - 1), 0)),
            pl.BlockSpec((dm, eblk), lambda i, j: (0, jnp.clip(j - 1, 0, n_blk - 1))),
            _const_spec((1, dm)),
        ],
        out_specs=pl.BlockSpec((tt, dm), lambda i, j: (i, 0)),
        out_shape=jax.ShapeDtypeStruct((tokens, dm), F32),
        scratch_shapes=[
            pltpu.VMEM((nch, dm, PEER_CHUNK), BF16),
            pltpu.VMEM((2 * PEER_HEADS, nlb, N_KEYS, LANES), F32),
            pltpu.VMEM((2, 3, PEER_TOPK, LANES), F32),
            pltpu.VMEM((PEER_HEADS, nlb, N_KEYS, LANES), F32),
            pltpu.VMEM((PEER_HEADS, nlb, N_KEYS, LANES), F32),
            pltpu.VMEM((PEER_HEADS, nlb, N_KEYS, LANES), F32),
            pltpu.VMEM((2, nch, eblk, PEER_CHUNK), F32),
            pltpu.VMEM((2, nch, eblk, PEER_CHUNK), BF16),
            pltpu.VMEM((nch, dm, PEER_CHUNK), F32),
        ],
        compiler_params=pltpu.CompilerParams(
            dimension_semantics=("arbitrary", "arbitrary"), vmem_limit_bytes=VMEM_LIMIT_BYTES),
    )(x, row(g), wq_t, keys, u_bf, u_bf, vt_bf, row(g_final))


def kernel(x, rms_mix_g, w_in, ret_gn_g, conv_dw_w, conv_dw_b, conv_ln_g, conv_ln_b, w_ret_proj, w_conv_proj,
           b_conv_proj, w_out, rms_ffn_g, w_query, peer_sub_keys, peer_u, peer_v, rms_final_g):
    bsz, seq, dm = x.shape
    depth = w_in.shape[0]
    for l in range(depth):
        x = _mixer(x, rms_mix_g[l], w_in[l], ret_gn_g[l], conv_dw_w[l], conv_dw_b[l], conv_ln_g[l],
                   conv_ln_b[l], w_ret_proj[l], w_conv_proj[l], b_conv_proj[l], w_out[l])
        x = _peer(x.reshape(bsz * seq, dm), rms_ffn_g[l], w_query[l], peer_sub_keys[l], peer_u[l], peer_v[l],
                  rms_final_g, final_norm=(l == depth - 1)).reshape(bsz, seq, dm)
    return x
```

```python
import functools

import numpy as np
import jax
import jax.numpy as jnp
from jax import lax
from jax.experimental import pallas as pl
from jax.experimental.pallas import tpu as pltpu

F32 = jnp.float32
BF16 = jnp.bfloat16

RET_HEADS = 4
RET_QK_DIM = 64
RET_V_DIM = 128
RET_QK = RET_HEADS * RET_QK_DIM
RET_WIDTH = RET_HEADS * RET_V_DIM
ROPE_BASE = 10000.0
CONV_CH = 512
CONV_K = 31
PEER_HEADS = 8
PEER_HALF = 128
N_KEYS = 128
PEER_TOPK = 16
RMS_EPS = 1e-6
LN_EPS = 1e-5
INV_SQRT2 = 0.7071067811865476

LANES = 128
SUBLANES = 8
VMEM_LIMIT_BYTES = 56 * 1024 * 1024
CONV_HIST = 32
CONV_ROWS = 32
PEER_CHUNK = 256
BUILD_ROWS = 32
BUILD_E1 = 4
NOT_SELECTED = 127.0


def _const_spec(shape):
    nd = len(shape)
    return pl.BlockSpec(shape, lambda *_: (0,) * nd, pipeline_mode=pl.Buffered(1))


def _swap_halves(t):
    half = RET_QK_DIM // 2
    parts = []
    for c in range(t.shape[1] // LANES):
        tc = t[:, c * LANES:(c + 1) * LANES]
        lane = lax.broadcasted_iota(jnp.int32, tc.shape, 1)
        first = (lane % RET_QK_DIM) < half
        nxt = pltpu.roll(tc, LANES - half, 1)
        prv = pltpu.roll(tc, half, 1)
        parts.append(jnp.where(first, nxt, prv))
    return jnp.concatenate(parts, axis=1)


def _mixer_kernel(x_ref, g_ref, win_ref, cos_ref, sin_ref, qd_ref, kd_ref, intra_ref, cd_ref, bd_ref,
                  hm_ref, gn_ref, dww_ref, dwb_ref, lng_ref, lnb_ref, wret_ref, wconv_ref, bconv_ref,
                  wout_ref, o_ref, state_ref, cbuf_ref, conv_ref, shift_ref, *, ts):
    @pl.when(pl.program_id(1) == 0)
    def _():
        state_ref[...] = jnp.zeros_like(state_ref)
        cbuf_ref[0:CONV_HIST, :] = jnp.zeros((CONV_HIST, CONV_CH), F32)

    x = x_ref[...]
    h = (x * lax.rsqrt(jnp.mean(x * x, axis=-1, keepdims=True) + RMS_EPS) * g_ref[...]).astype(BF16)

    c0 = 2 * RET_QK + RET_WIDTH
    qkv = jnp.dot(h, win_ref[:, 0:c0], preferred_element_type=F32)
    q = qkv[:, 0:RET_QK]
    k = qkv[:, RET_QK:2 * RET_QK]
    vb = qkv[:, 2 * RET_QK:c0].astype(BF16)
    cos = cos_ref[...]
    sin = sin_ref[...]
    qr = q * cos + _swap_halves(q) * sin
    kr = (k * cos + _swap_halves(k) * sin) * (RET_QK_DIM ** -0.5)
    qb = qr.astype(BF16)
    kbt = kr.T.astype(BF16)
    inner = []
    for hh in range(RET_HEADS):
        qm = (qr * hm_ref[hh:hh + 1, :]).astype(BF16)
        sc = jnp.dot(qm, kbt, preferred_element_type=F32)
        sc = sc * intra_ref[hh]
        inner.append(jnp.dot(sc.astype(BF16), vb[:, hh * RET_V_DIM:(hh + 1) * RET_V_DIM],
                             preferred_element_type=F32))
    st = state_ref[...]
    cross = jnp.dot(qb, st.astype(BF16), preferred_element_type=F32) * qd_ref[...]
    o = jnp.concatenate(inner, axis=1) + cross
    kv = lax.dot_general((kr * kd_ref[...]).astype(BF16), vb, (((0,), (0,)), ((), ())),
                         preferred_element_type=F32)
    state_ref[...] = st * cd_ref[...] + kv * bd_ref[...]

    normed = []
    for hh in range(RET_HEADS):
        oh = o[:, hh * RET_V_DIM:(hh + 1) * RET_V_DIM]
        d = oh - jnp.mean(oh, axis=-1, keepdims=True)
        normed.append(d * lax.rsqrt(jnp.mean(d * d, axis=-1, keepdims=True) + LN_EPS))
    on = jnp.concatenate(normed, axis=1) * gn_ref[...]

    c1 = c0 + RET_WIDTH + 2 * CONV_CH
    p2 = jnp.dot(h, win_ref[:, c0:c1], preferred_element_type=F32)
    g_ret = p2[:, 0:RET_WIDTH]
    yret = (g_ret * jax.nn.sigmoid(g_ret) * on).astype(BF16)
    a_out = jnp.dot(yret, wret_ref[...], preferred_element_type=F32)

    u_val = p2[:, RET_WIDTH:RET_WIDTH + CONV_CH]
    u_gate = p2[:, RET_WIDTH + CONV_CH:RET_WIDTH + 2 * CONV_CH]
    cbuf_ref[CONV_HIST:CONV_HIST + ts, :] = u_val * jax.nn.sigmoid(u_gate)
    base = CONV_HIST - (CONV_K - 1)
    groups = [[o for o in range(base, base + CONV_K) if o % SUBLANES == r] for r in range(SUBLANES)]
    for r in range(1, SUBLANES):
        offs = groups[r]
        n_rows = ts + offs[-1] - offs[0]
        shift_ref[r, 0:n_rows, :] = cbuf_ref[pl.ds(offs[0], n_rows), :]
    for rc in range(ts // CONV_ROWS):
        acc = jnp.broadcast_to(dwb_ref[...], (CONV_ROWS, CONV_CH))
        for r in range(SUBLANES):
            for o in groups[r]:
                if r == 0:
                    tap = cbuf_ref[pl.ds(rc * CONV_ROWS + o, CONV_ROWS), :]
                else:
                    tap = shift_ref[r, pl.ds(rc * CONV_ROWS + o - groups[r][0], CONV_ROWS), :]
                acc = acc + dww_ref[o - base:o - base + 1, :] * tap
        conv_ref[rc * CONV_ROWS:(rc + 1) * CONV_ROWS, :] = acc
    cbuf_ref[0:CONV_HIST, :] = cbuf_ref[ts:ts + CONV_HIST, :]
    cv = conv_ref[...]
    d = cv - jnp.mean(cv, axis=-1, keepdims=True)
    ln = d * lax.rsqrt(jnp.mean(d * d, axis=-1, keepdims=True) + LN_EPS) * lng_ref[...] + lnb_ref[...]
    yconv = (ln * jax.nn.sigmoid(ln)).astype(BF16)
    b_out = jnp.dot(yconv, wconv_ref[...], preferred_element_type=F32) + bconv_ref[...]

    p3 = jnp.dot(h, win_ref[:, c1:], preferred_element_type=F32)
    dm = x.shape[1]
    mixed = jax.nn.sigmoid(p3[:, 0:dm]) * a_out + jax.nn.sigmoid(p3[:, dm:2 * dm]) * b_out
    o_ref[...] = x + jnp.dot(mixed.astype(BF16), wout_ref[...], preferred_element_type=F32)


def _mixer_tables(seq, ts):
    half = RET_QK_DIM // 2
    inv = 1.0 / (ROPE_BASE ** (np.arange(half, dtype=np.float64) / half))
    ang = np.arange(seq, dtype=np.float64)[:, None] * inv[None, :]
    cos = np.tile(np.concatenate([np.cos(ang), np.cos(ang)], axis=1), (1, RET_HEADS))
    sin = np.tile(np.concatenate([-np.sin(ang), np.sin(ang)], axis=1), (1, RET_HEADS))
    log_gamma = np.log1p(-(2.0 ** (-5.0 - np.arange(RET_HEADS, dtype=np.float64))))
    n = np.arange(ts, dtype=np.float64)
    diff = n[:, None] - n[None, :]
    intra = np.where(diff[None] >= 0, np.exp(log_gamma[:, None, None] * np.maximum(diff, 0.0)[None]), 0.0)
    q_decay = np.exp(log_gamma[:, None] * (n + 1.0)[None])
    k_decay = np.exp(log_gamma[:, None] * (ts - 1.0 - n)[None])
    chunk_decay = np.exp(log_gamma * ts)
    qd = np.repeat(q_decay.T, RET_V_DIM, axis=1)
    kd = np.repeat(k_decay.T, RET_QK_DIM, axis=1)
    cd = np.repeat(chunk_decay, RET_V_DIM)[None, :]
    row_head = np.arange(RET_QK) // RET_QK_DIM
    col_head = np.arange(RET_WIDTH) // RET_V_DIM
    bd = (row_head[:, None] == col_head[None, :]).astype(np.float64)
    hm = (np.arange(RET_HEADS)[:, None] == row_head[None, :]).astype(np.float64)
    as32 = lambda a: jnp.asarray(a, dtype=F32)
    return tuple(as32(a) for a in (cos, sin, qd, kd, intra, cd, bd, hm))


def _mixer_tile(seq):
    for ts in (256, 128, 64):
        if seq % ts == 0:
            return ts
    raise ValueError(f"sequence length {seq} must be a multiple of {CONV_ROWS}")


def _mixer(x, g, w_in, gn_g, dw_w, dw_b, ln_g, ln_b, w_ret, w_conv, b_conv, w_out):
    bsz, seq, dm = x.shape
    ts = _mixer_tile(seq)
    cos, sin, qd, kd, intra, cd, bd, hm = _mixer_tables(seq, ts)
    row = lambda a: a.reshape(1, -1).astype(F32)
    in_cols = w_in.shape[1]
    grid = (bsz, seq // ts)
    in_specs = [
        pl.BlockSpec((None, ts, dm), lambda b, j: (b, j, 0)),
        _const_spec((1, dm)),
        _const_spec((dm, in_cols)),
        pl.BlockSpec((ts, RET_QK), lambda b, j: (j, 0)),
        pl.BlockSpec((ts, RET_QK), lambda b, j: (j, 0)),
        _const_spec((ts, RET_WIDTH)),
        _const_spec((ts, RET_QK)),
        _const_spec((RET_HEADS, ts, ts)),
        _const_spec((1, RET_WIDTH)),
        _const_spec((RET_QK, RET_WIDTH)),
        _const_spec((RET_HEADS, RET_QK)),
        _const_spec((1, RET_WIDTH)),
        _const_spec((CONV_K, CONV_CH)),
        _const_spec((1, CONV_CH)),
        _const_spec((1, CONV_CH)),
        _const_spec((1, CONV_CH)),
        _const_spec((RET_WIDTH, dm)),
        _const_spec((CONV_CH, dm)),
        _const_spec((1, dm)),
        _const_spec((dm, dm)),
    ]
    return pl.pallas_call(
        functools.partial(_mixer_kernel, ts=ts),
        name="mixer",
        grid=grid,
        in_specs=in_specs,
        out_specs=pl.BlockSpec((None, ts, dm), lambda b, j: (b, j, 0)),
        out_shape=jax.ShapeDtypeStruct((bsz, seq, dm), F32),
        scratch_shapes=[
            pltpu.VMEM((RET_QK, RET_WIDTH), F32),
            pltpu.VMEM((CONV_HIST + ts, CONV_CH), F32),
            pltpu.VMEM((ts, CONV_CH), F32),
            pltpu.VMEM((SUBLANES, ts + CONV_HIST - SUBLANES, CONV_CH), F32),
        ],
        compiler_params=pltpu.CompilerParams(
            dimension_semantics=("arbitrary", "arbitrary"), vmem_limit_bytes=VMEM_LIMIT_BYTES),
    )(x, row(g), w_in.astype(BF16), cos, sin, qd, kd, intra, cd, bd, hm, row(gn_g), dw_w.astype(F32),
      row(dw_b), row(ln_g), row(ln_b), w_ret.astype(BF16), w_conv.astype(BF16), row(b_conv),
      w_out.astype(BF16))


def _zip_steps(*steppers):
    results = [None] * len(steppers)
    live = dict(enumerate(steppers))
    while live:
        for idx in list(live):
            try:
                next(live[idx])
            except StopIteration as done:
                results[idx] = done.value
                del live[idx]
        if live:
            yield
    return results


def _interleave(*steppers):
    zipped = _zip_steps(*steppers)
    try:
        while True:
            next(zipped)
    except StopIteration as done:
        return done.value


def _top16(s, row_id, break_ties, want_rank=True):
    cur = s
    rank = jnp.full(s.shape, NOT_SELECTED, F32) if want_rank else None
    vals = []
    for r in range(PEER_TOPK):
        m = jnp.max(cur, axis=0, keepdims=True)
        if break_ties:
            first = jnp.min(jnp.where(cur == m, row_id, 1e9), axis=0, keepdims=True)
            hit = row_id == first
        else:
            hit = cur == m
        if want_rank:
            rank = jnp.where(hit, float(r), rank)
        cur = jnp.where(hit, -jnp.inf, cur)
        vals.append(m)
        yield
    return jnp.concatenate(vals, axis=0), rank


def _merge_exchange_network(n):
    pairs = []
    t = n.bit_length() - 1
    p = 1 << (t - 1)
    while p > 0:
        q, r, d = 1 << (t - 1), 0, p
        while d > 0:
            pairs += [(i, i + d) for i in range(n - d) if (i & p) == r]
            d, q, r = q - p, q >> 1, p
        p >>= 1
    return pairs


def _top16_values(rows):
    k = PEER_TOPK
    n_real = len(rows)
    assert n_real <= k
    x = list(rows) + [None] * (k - n_real)
    for n, (i, j) in enumerate(_merge_exchange_network(k)):
        if x[i] is None:
            x[i], x[j] = x[j], None
        elif x[j] is not None:
            x[i], x[j] = jnp.maximum(x[i], x[j]), jnp.minimum(x[i], x[j])
        if n % 8 == 7:
            yield
    x = [jnp.full((SUBLANES, LANES), -jnp.inf, F32) if xi is None else xi for xi in x]
    shift = SUBLANES // 2
    while shift >= 1:
        x = [jnp.maximum(x[i], pltpu.roll(x[k - 1 - i], shift, 0)) for i in range(k)]
        yield
        stride = k // 2
        while stride >= 1:
            for i in range(k):
                if (i & stride) == 0:
                    x[i], x[i + stride] = jnp.maximum(x[i], x[i + stride]), jnp.minimum(x[i], x[i + stride])
            stride //= 2
            yield
        shift //= 2
    dup = jnp.zeros((SUBLANES, LANES), F32)
    for i in range(k - 1):
        dup = jnp.where(x[i] == x[i + 1], 1.0, dup)
    row = lax.broadcasted_iota(jnp.int32, (SUBLANES, LANES), 0)
    halves = []
    for h0 in (0, SUBLANES):
        v = x[h0]
        for i in range(1, SUBLANES):
            v = jnp.where(row == i, x[h0 + i], v)
        halves.append(v)
    return jnp.concatenate(halves, axis=0), dup


def _staircase(v1, v2, e1x, e2x, break_ties):
    k = PEER_TOPK
    row = lax.broadcasted_iota(jnp.int32, (SUBLANES, LANES), 0).astype(F32)
    ninf = jnp.full((SUBLANES, LANES), -jnp.inf, F32)
    cand, pos, wgt = [], [], []
    for b0 in (0, SUBLANES):
        cand.append(v1[0:1] + v2[b0:b0 + SUBLANES]); pos.append(row + float(b0))
        wgt.append(e1x[0:1] * e2x[b0:b0 + SUBLANES])
    cand.append(v1[1:2] + v2[0:SUBLANES]); pos.append(row + float(k)); wgt.append(e1x[1:2] * e2x[0:SUBLANES])
    cand.append(v1[SUBLANES:k] + v2[0:1]); pos.append((row + float(SUBLANES)) * float(k))
    wgt.append(e1x[SUBLANES:k] * e2x[0:1])
    n_col = 5
    for b in range(n_col):
        a_max = float(k // (b + 1) - 1)
        ok = (row >= 2.0) & (row <= a_max)
        cand.append(jnp.where(ok, v1[0:SUBLANES] + v2[b:b + 1], ninf))
        pos.append(jnp.where(ok, row * float(k) + float(b), -1.0))
        wgt.append(e1x[0:SUBLANES] * e2x[b:b + 1])
    ng = len(cand)
    sel = [jnp.zeros((SUBLANES, LANES), F32) for _ in range(ng)]
    if not break_ties:
        best, _ = yield from _top16_values(cand)
        sel = [jnp.where(c >= best[k - 1:k], 1.0, 0.0) for c in cand]
    for _ in range(k if break_ties else 0):
        m = cand[0]
        for g in range(1, ng):
            m = jnp.maximum(m, cand[g])
        m = jnp.max(m, axis=0, keepdims=True)
        fp = jnp.where(cand[0] == m, pos[0], 1e9)
        for g in range(1, ng):
            fp = jnp.minimum(fp, jnp.where(cand[g] == m, pos[g], 1e9))
        fp = jnp.min(fp, axis=0, keepdims=True)
        for g in range(ng):
            hit = pos[g] == fp
            sel[g] = jnp.where(hit, 1.0, sel[g])
            cand[g] = jnp.where(hit, -jnp.inf, cand[g])
        yield
    b0cnt = jnp.sum(sel[0] + sel[1], axis=0, keepdims=True)
    b1cnt = jnp.sum(sel[2], axis=0, keepdims=True)
    low = sel[4]
    for g in range(5, ng):
        low = low + sel[g]
    low = low + jnp.where(row == 0.0, b0cnt, 0.0) + jnp.where(row == 1.0, b1cnt, 0.0)
    counts = jnp.concatenate([low, sel[3]], axis=0)
    zt = sel[0] * wgt[0]
    nsel = sel[0]
    for g in range(1, ng):
        zt = zt + sel[g] * wgt[g]
        nsel = nsel + sel[g]
    return counts, jnp.sum(zt, axis=0, keepdims=True), jnp.sum(nsel, axis=0, keepdims=True)


def _peer_kernel(x_ref, g_ref, wq_ref, keys_ref, u0_ref, u_ref, vt_ref, gfin_ref, o_ref,
                 hn_ref, s_ref, vbuf_ref, w2_ref, thr_ref, w1_ref, st_ref, at_ref, acc_ref,
                 *, tt, eb, final_norm):
    j = pl.program_id(1)
    n_blk = pl.num_programs(1) - 1
    nch = tt // PEER_CHUNK
    lpc = PEER_CHUNK // LANES
    qd = 2 * PEER_HALF

    @pl.when(j == 0)
    def _routing():
        x = x_ref[...]
        hn = x * lax.rsqrt(jnp.mean(x * x, axis=-1, keepdims=True) + RMS_EPS) * g_ref[...]
        for c in range(nch):
            hn_c = hn[c * PEER_CHUNK:(c + 1) * PEER_CHUNK, :].T.astype(BF16)
            hn_ref[c] = hn_c
            acc_ref[c] = jnp.zeros(acc_ref.shape[1:], F32)
            at_ref[1, c] = jnp.zeros(at_ref.shape[2:], BF16)
            st_ref[0, c] = jnp.dot(u0_ref[...], hn_c, preferred_element_type=F32)
            for hh in range(PEER_HEADS):
                qt = jnp.dot(wq_ref[hh * qd:(hh + 1) * qd, :], hn_c, preferred_element_type=F32).astype(BF16)
                for p in range(2):
                    sc = jnp.dot(keys_ref[2 * hh + p], qt[p * PEER_HALF:(p + 1) * PEER_HALF, :],
                                 preferred_element_type=F32)
                    for l in range(lpc):
                        s_ref[2 * hh + p, c * lpc + l] = sc[:, l * LANES:(l + 1) * LANES]

        row_id = lax.broadcasted_iota(jnp.int32, (N_KEYS, LANES), 0).astype(F32)

        n_route = PEER_HEADS * nch * lpc
        topk = float(PEER_TOPK)

        def scores(blk):
            hh = blk // (nch * lpc)
            lb = blk % (nch * lpc)
            return hh, lb, s_ref[2 * hh, lb], s_ref[2 * hh + 1, lb]

        def dense_out(hh, lb, s1, s2, v1, v2, z, thr):
            thr_ref[hh, lb] = thr
            w1_ref[hh, lb] = jnp.exp(s1 - v1[0:1]) * (0.5 / z)
            w2_ref[hh, lb] = jnp.exp(s2 - v2[0:1])

        def fast_a(blk, slot):
            hh, lb, s1, s2 = scores(blk)
            groups = lambda s: [s[i * SUBLANES:(i + 1) * SUBLANES, :] for i in range(N_KEYS // SUBLANES)]
            (v1, dup1), (v2, dup2) = yield from _zip_steps(_top16_values(groups(s1)), _top16_values(groups(s2)))
            last = slice(PEER_TOPK - 1, PEER_TOPK)
            in_top = jnp.where(s1 >= v1[last], 1.0, 0.0) + jnp.where(s2 >= v2[last], 1.0, 0.0)
            vbuf_ref[slot, 0] = v1
            vbuf_ref[slot, 1] = v2
            vbuf_ref[slot, 2, 0:1, :] = (jnp.sum(in_top, axis=0, keepdims=True)
                                         + jnp.max(dup1 + dup2, axis=0, keepdims=True))

        def fast_b(blk, slot):
            hh, lb, s1, s2 = scores(blk)
            v1 = vbuf_ref[slot, 0]
            v2 = vbuf_ref[slot, 1]
            counts, z, nsel = yield from _staircase(v1, v2, jnp.exp(v1 - v1[0:1]), jnp.exp(v2 - v2[0:1]), False)
            tval = jnp.full((PEER_TOPK, LANES), jnp.inf, F32)
            for b in range(PEER_TOPK):
                tval = jnp.where(counts == float(b + 1), v2[b:b + 1], tval)
            thr = jnp.full((N_KEYS, LANES), jnp.inf, F32)
            for a in range(SUBLANES):
                thr = jnp.where(s1 == v1[a:a + 1], tval[a:a + 1], thr)
            lo = jnp.min(jnp.where(counts[SUBLANES:] > 0.0, v1[SUBLANES:], jnp.inf), axis=0, keepdims=True)
            thr = jnp.where(s1 < v1[SUBLANES - 1:SUBLANES], jnp.where(s1 >= lo, v2[0:1], thr), thr)
            dense_out(hh, lb, s1, s2, v1, v2, z, thr)
            return vbuf_ref[slot, 2, 0:1, :] + nsel

        def exact(blk):
            hh, lb, s1, s2 = scores(blk)
            (v1, rank1), (v2, rank2) = _interleave(_top16(s1, row_id, True), _top16(s2, row_id, True))
            (counts, z, _), = _interleave(
                _staircase(v1, v2, jnp.exp(v1 - v1[0:1]), jnp.exp(v2 - v2[0:1]), True))
            bcnt = jnp.zeros((N_KEYS, LANES), F32)
            for a in range(PEER_TOPK):
                bcnt = jnp.where(rank1 == float(a), counts[a:a + 1], bcnt)
            dense_out(hh, lb, s1, s2, v1, v2, z, jnp.where(bcnt > 0.0, 1.0 - bcnt, jnp.inf))
            s_ref[2 * hh + 1, lb] = -rank2

        _interleave(fast_a(0, 0))

        def route(i, carry):
            slot = i % 2
            _, picked = _interleave(fast_a(jnp.minimum(i + 1, n_route - 1), 1 - slot), fast_b(i, slot))
            tied = jnp.max(jnp.where(picked == 3.0 * topk, 0.0, 1.0))

            @pl.when(tied > 0.0)
            def _():
                exact(i)

            return carry

        lax.fori_loop(0, n_route, route, 0)

    def steady(cur, nxt):
        def chunk(c, carry):
            acc_ref[c] += jnp.dot(vt_ref[...], at_ref[nxt, c], preferred_element_type=F32)
            for e0 in range(0, eb, BUILD_E1):
                for l in range(lpc):
                    lb = c * lpc + l
                    cols = slice(l * LANES, (l + 1) * LANES)
                    for g in range(N_KEYS // BUILD_ROWS):
                        grow = slice(g * BUILD_ROWS, (g + 1) * BUILD_ROWS)
                        gates = [jnp.zeros((BUILD_ROWS, LANES), F32) for _ in range(BUILD_E1)]
                        for hh in range(PEER_HEADS):
                            key = s_ref[2 * hh + 1, lb, grow, :]
                            w2 = w2_ref[hh, lb, grow, :]
                            for k in range(BUILD_E1):
                                e1 = j * eb + e0 + k
                                thr = thr_ref[hh, lb, pl.ds(e1, 1), :]
                                w1 = w1_ref[hh, lb, pl.ds(e1, 1), :]
                                gates[k] = gates[k] + jnp.where(key >= thr, w2, 0.0) * w1
                        for k in range(BUILD_E1):
                            rows = slice((e0 + k) * N_KEYS + g * BUILD_ROWS, (e0 + k) * N_KEYS + (g + 1) * BUILD_ROWS)
                            xs = st_ref[cur, c, rows, cols]
                            ge = xs * (1.0 + lax.erf(xs * INV_SQRT2))
                            at_ref[cur, c, rows, cols] = (ge * gates[k]).astype(BF16)
            st_ref[nxt, c] = jnp.dot(u_ref[...], hn_ref[c], preferred_element_type=F32)
            return carry

        lax.fori_loop(0, nch, chunk, 0)

    @pl.when((j < n_blk) & (j % 2 == 0))
    def _even():
        steady(0, 1)

    @pl.when((j < n_blk) & (j % 2 == 1))
    def _odd():
        steady(1, 0)

    @pl.when(j == n_blk)
    def _finish():
        last = (N_KEYS // eb - 1) % 2
        for c in range(nch):
            acc = acc_ref[c] + jnp.dot(vt_ref[...], at_ref[last, c], preferred_element_type=F32)
            rows = slice(c * PEER_CHUNK, (c + 1) * PEER_CHUNK)
            y = acc.T + x_ref[rows, :]
            if final_norm:
                y = y * lax.rsqrt(jnp.mean(y * y, axis=-1, keepdims=True) + RMS_EPS) * gfin_ref[...]
            o_ref[rows, :] = y


def _peer_tiles(tokens):
    for tt in (512, 256):
        if tokens % tt == 0:
            return tt, 8
    raise ValueError(f"token count {tokens} must be a multiple of {PEER_CHUNK}")


def _peer(x, g, w_query, sub_keys, u_tab, v_tab, g_final, final_norm):
    tokens, dm = x.shape
    tt, eb = _peer_tiles(tokens)
    nlb = tt // LANES
    nch = tt // PEER_CHUNK
    eblk = eb * N_KEYS
    n_exp = u_tab.shape[0]
    assert n_exp == N_KEYS * N_KEYS and N_KEYS % eb == 0
    assert sub_keys.shape == (PEER_HEADS, 2, N_KEYS, PEER_HALF)
    n_blk = N_KEYS // eb
    wq_t = w_query.T.astype(BF16)
    keys = sub_keys.reshape(PEER_HEADS * 2, N_KEYS, PEER_HALF).astype(BF16)
    u_bf = u_tab.astype(BF16)
    vt_bf = v_tab.T.astype(BF16)
    row = lambda a: a.reshape(1, -1).astype(F32)
    grid = (tokens // tt, n_blk + 1)
    return pl.pallas_call(
        functools.partial(_peer_kernel, tt=tt, eb=eb, final_norm=final_norm),
        name="peer",
        grid=grid,
        in_specs=[
            pl.BlockSpec((tt, dm), lambda i, j: (i, 0)),
            _const_spec((1, dm)),
            _const_spec(wq_t.shape),
            _const_spec(keys.shape),
            _const_spec((eblk, dm)),
            pl.BlockSpec((eblk, dm), lambda i, j: (jnp.minimum(j + 1, n_blk - 1), 0)),
            pl.BlockSpec((dm, eblk), lambda i, j: (0, jnp.clip(j - 1, 0, n_blk - 1))),
            _const_spec((1, dm)),
        ],
        out_specs=pl.BlockSpec((tt, dm), lambda i, j: (i, 0)),
        out_shape=jax.ShapeDtypeStruct((tokens, dm), F32),
        scratch_shapes=[
            pltpu.VMEM((nch, dm, PEER_CHUNK), BF16),
            pltpu.VMEM((2 * PEER_HEADS, nlb, N_KEYS, LANES), F32),
            pltpu.VMEM((2, 3, PEER_TOPK, LANES), F32),
            pltpu.VMEM((PEER_HEADS, nlb, N_KEYS, LANES), F32),
            pltpu.VMEM((PEER_HEADS, nlb, N_KEYS, LANES), F32),
            pltpu.VMEM((PEER_HEADS, nlb, N_KEYS, LANES), F32),
            pltpu.VMEM((2, nch, eblk, PEER_CHUNK), F32),
            pltpu.VMEM((2, nch, eblk, PEER_CHUNK), BF16),
            pltpu.VMEM((nch, dm, PEER_CHUNK), F32),
        ],
        compiler_params=pltpu.CompilerParams(
            dimension_semantics=("arbitrary", "arbitrary"), vmem_limit_bytes=VMEM_LIMIT_BYTES),
    )(x, row(g), wq_t, keys, u_bf, u_bf, vt_bf, row(g_final))


def kernel(x, rms_mix_g, w_in, ret_gn_g, conv_dw_w, conv_dw_b, conv_ln_g, conv_ln_b, w_ret_proj, w_conv_proj,
           b_conv_proj, w_out, rms_ffn_g, w_query, peer_sub_keys, peer_u, peer_v, rms_final_g):
    bsz, seq, dm = x.shape
    depth = w_in.shape[0]
    for l in range(depth):
        x = _mixer(x, rms_mix_g[l], w_in[l], ret_gn_g[l], conv_dw_w[l], conv_dw_b[l], conv_ln_g[l],
                   conv_ln_b[l], w_ret_proj[l], w_conv_proj[l], b_conv_proj[l], w_out[l])
        x = _peer(x.reshape(bsz * seq, dm), rms_ffn_g[l], w_query[l], peer_sub_keys[l], peer_u[l], peer_v[l],
                  rms_final_g, final_norm=(l == depth - 1)).reshape(bsz, seq, dm)
    return x
```

```python
import functools

import numpy as np
import jax
import jax.numpy as jnp
from jax import lax
from jax.experimental import pallas as pl
from jax.experimental.pallas import tpu as pltpu

F32 = jnp.float32
BF16 = jnp.bfloat16

RET_HEADS = 4
RET_QK_DIM = 64
RET_V_DIM = 128
RET_QK = RET_HEADS * RET_QK_DIM
RET_WIDTH = RET_HEADS * RET_V_DIM
ROPE_BASE = 10000.0
CONV_CH = 512
CONV_K = 31
PEER_HEADS = 8
PEER_HALF = 128
N_KEYS = 128
PEER_TOPK = 16
RMS_EPS = 1e-6
LN_EPS = 1e-5
INV_SQRT2 = 0.7071067811865476

LANES = 128
SUBLANES = 8
VMEM_LIMIT_BYTES = 56 * 1024 * 1024
CONV_HIST = 32
CONV_ROWS = 32
PEER_CHUNK = 256
BUILD_ROWS = 32
BUILD_E1 = 4
NOT_SELECTED = 127.0


def _const_spec(shape):
    nd = len(shape)
    return pl.BlockSpec(shape, lambda *_: (0,) * nd, pipeline_mode=pl.Buffered(1))


def _swap_halves(t):
    half = RET_QK_DIM // 2
    parts = []
    for c in range(t.shape[1] // LANES):
        tc = t[:, c * LANES:(c + 1) * LANES]
        lane = lax.broadcasted_iota(jnp.int32, tc.shape, 1)
        first = (lane % RET_QK_DIM) < half
        nxt = pltpu.roll(tc, LANES - half, 1)
        prv = pltpu.roll(tc, half, 1)
        parts.append(jnp.where(first, nxt, prv))
    return jnp.concatenate(parts, axis=1)


def _mixer_kernel(x_ref, g_ref, win_ref, cos_ref, sin_ref, qd_ref, kd_ref, intra_ref, cd_ref, bd_ref,
                  hm_ref, gn_ref, dww_ref, dwb_ref, lng_ref, lnb_ref, wret_ref, wconv_ref, bconv_ref,
                  wout_ref, o_ref, state_ref, cbuf_ref, conv_ref, shift_ref, *, ts):
    @pl.when(pl.program_id(1) == 0)
    def _():
        state_ref[...] = jnp.zeros_like(state_ref)
        cbuf_ref[0:CONV_HIST, :] = jnp.zeros((CONV_HIST, CONV_CH), F32)

    x = x_ref[...]
    h = (x * lax.rsqrt(jnp.mean(x * x, axis=-1, keepdims=True) + RMS_EPS) * g_ref[...]).astype(BF16)

    c0 = 2 * RET_QK + RET_WIDTH
    qkv = jnp.dot(h, win_ref[:, 0:c0], preferred_element_type=F32)
    q = qkv[:, 0:RET_QK]
    k = qkv[:, RET_QK:2 * RET_QK]
    vb = qkv[:, 2 * RET_QK:c0].astype(BF16)
    cos = cos_ref[...]
    sin = sin_ref[...]
    qr = q * cos + _swap_halves(q) * sin
    kr = (k * cos + _swap_halves(k) * sin) * (RET_QK_DIM ** -0.5)
    qb = qr.astype(BF16)
    kbt = kr.T.astype(BF16)
    inner = []
    for hh in range(RET_HEADS):
        qm = (qr * hm_ref[hh:hh + 1, :]).astype(BF16)
        sc = jnp.dot(qm, kbt, preferred_element_type=F32)
        sc = sc * intra_ref[hh]
        inner.append(jnp.dot(sc.astype(BF16), vb[:, hh * RET_V_DIM:(hh + 1) * RET_V_DIM],
                             preferred_element_type=F32))
    st = state_ref[...]
    cross = jnp.dot(qb, st.astype(BF16), preferred_element_type=F32) * qd_ref[...]
    o = jnp.concatenate(inner, axis=1) + cross
    kv = lax.dot_general((kr * kd_ref[...]).astype(BF16), vb, (((0,), (0,)), ((), ())),
                         preferred_element_type=F32)
    state_ref[...] = st * cd_ref[...] + kv * bd_ref[...]

    normed = []
    for hh in range(RET_HEADS):
        oh = o[:, hh * RET_V_DIM:(hh + 1) * RET_V_DIM]
        d = oh - jnp.mean(oh, axis=-1, keepdims=True)
        normed.append(d * lax.rsqrt(jnp.mean(d * d, axis=-1, keepdims=True) + LN_EPS))
    on = jnp.concatenate(normed, axis=1) * gn_ref[...]

    c1 = c0 + RET_WIDTH + 2 * CONV_CH
    p2 = jnp.dot(h, win_ref[:, c0:c1], preferred_element_type=F32)
    g_ret = p2[:, 0:RET_WIDTH]
    yret = (g_ret * jax.nn.sigmoid(g_ret) * on).astype(BF16)
    a_out = jnp.dot(yret, wret_ref[...], preferred_element_type=F32)

    u_val = p2[:, RET_WIDTH:RET_WIDTH + CONV_CH]
    u_gate = p2[:, RET_WIDTH + CONV_CH:RET_WIDTH + 2 * CONV_CH]
    cbuf_ref[CONV_HIST:CONV_HIST + ts, :] = u_val * jax.nn.sigmoid(u_gate)
    base = CONV_HIST - (CONV_K - 1)
    groups = [[o for o in range(base, base + CONV_K) if o % SUBLANES == r] for r in range(SUBLANES)]
    for r in range(1, SUBLANES):
        offs = groups[r]
        n_rows = ts + offs[-1] - offs[0]
        shift_ref[r, 0:n_rows, :] = cbuf_ref[pl.ds(offs[0], n_rows), :]
    for rc in range(ts // CONV_ROWS):
        acc = jnp.broadcast_to(dwb_ref[...], (CONV_ROWS, CONV_CH))
        for r in range(SUBLANES):
            for o in groups[r]:
                if r == 0:
                    tap = cbuf_ref[pl.ds(rc * CONV_ROWS + o, CONV_ROWS), :]
                else:
                    tap = shift_ref[r, pl.ds(rc * CONV_ROWS + o - groups[r][0], CONV_ROWS), :]
                acc = acc + dww_ref[o - base:o - base + 1, :] * tap
        conv_ref[rc * CONV_ROWS:(rc + 1) * CONV_ROWS, :] = acc
    cbuf_ref[0:CONV_HIST, :] = cbuf_ref[ts:ts + CONV_HIST, :]
    cv = conv_ref[...]
    d = cv - jnp.mean(cv, axis=-1, keepdims=True)
    ln = d * lax.rsqrt(jnp.mean(d * d, axis=-1, keepdims=True) + LN_EPS) * lng_ref[...] + lnb_ref[...]
    yconv = (ln * jax.nn.sigmoid(ln)).astype(BF16)
    b_out = jnp.dot(yconv, wconv_ref[...], preferred_element_type=F32) + bconv_ref[...]

    p3 = jnp.dot(h, win_ref[:, c1:], preferred_element_type=F32)
    dm = x.shape[1]
    mixed = jax.nn.sigmoid(p3[:, 0:dm]) * a_out + jax.nn.sigmoid(p3[:, dm:2 * dm]) * b_out
    o_ref[...] = x + jnp.dot(mixed.astype(BF16), wout_ref[...], preferred_element_type=F32)


def _mixer_tables(seq, ts):
    half = RET_QK_DIM // 2
    inv = 1.0 / (ROPE_BASE ** (np.arange(half, dtype=np.float64) / half))
    ang = np.arange(seq, dtype=np.float64)[:, None] * inv[None, :]
    cos = np.tile(np.concatenate([np.cos(ang), np.cos(ang)], axis=1), (1, RET_HEADS))
    sin = np.tile(np.concatenate([-np.sin(ang), np.sin(ang)], axis=1), (1, RET_HEADS))
    log_gamma = np.log1p(-(2.0 ** (-5.0 - np.arange(RET_HEADS, dtype=np.float64))))
    n = np.arange(ts, dtype=np.float64)
    diff = n[:, None] - n[None, :]
    intra = np.where(diff[None] >= 0, np.exp(log_gamma[:, None, None] * np.maximum(diff, 0.0)[None]), 0.0)
    q_decay = np.exp(log_gamma[:, None] * (n + 1.0)[None])
    k_decay = np.exp(log_gamma[:, None] * (ts - 1.0 - n)[None])
    chunk_decay = np.exp(log_gamma * ts)
    qd = np.repeat(q_decay.T, RET_V_DIM, axis=1)
    kd = np.repeat(k_decay.T, RET_QK_DIM, axis=1)
    cd = np.repeat(chunk_decay, RET_V_DIM)[None, :]
    row_head = np.arange(RET_QK) // RET_QK_DIM
    col_head = np.arange(RET_WIDTH) // RET_V_DIM
    bd = (row_head[:, None] == col_head[None, :]).astype(np.float64)
    hm = (np.arange(RET_HEADS)[:, None] == row_head[None, :]).astype(np.float64)
    as32 = lambda a: jnp.asarray(a, dtype=F32)
    return tuple(as32(a) for a in (cos, sin, qd, kd, intra, cd, bd, hm))


def _mixer_tile(seq):
    for ts in (256, 128, 64):
        if seq % ts == 0:
            return ts
    raise ValueError(f"sequence length {seq} must be a multiple of {CONV_ROWS}")


def _mixer(x, g, w_in, gn_g, dw_w, dw_b, ln_g, ln_b, w_ret, w_conv, b_conv, w_out):
    bsz, seq, dm = x.shape
    ts = _mixer_tile(seq)
    cos, sin, qd, kd, intra, cd, bd, hm = _mixer_tables(seq, ts)
    row = lambda a: a.reshape(1, -1).astype(F32)
    in_cols = w_in.shape[1]
    grid = (bsz, seq // ts)
    in_specs = [
        pl.BlockSpec((None, ts, dm), lambda b, j: (b, j, 0)),
        _const_spec((1, dm)),
        _const_spec((dm, in_cols)),
        pl.BlockSpec((ts, RET_QK), lambda b, j: (j, 0)),
        pl.BlockSpec((ts, RET_QK), lambda b, j: (j, 0)),
        _const_spec((ts, RET_WIDTH)),
        _const_spec((ts, RET_QK)),
        _const_spec((RET_HEADS, ts, ts)),
        _const_spec((1, RET_WIDTH)),
        _const_spec((RET_QK, RET_WIDTH)),
        _const_spec((RET_HEADS, RET_QK)),
        _const_spec((1, RET_WIDTH)),
        _const_spec((CONV_K, CONV_CH)),
        _const_spec((1, CONV_CH)),
        _const_spec((1, CONV_CH)),
        _const_spec((1, CONV_CH)),
        _const_spec((RET_WIDTH, dm)),
        _const_spec((CONV_CH, dm)),
        _const_spec((1, dm)),
        _const_spec((dm, dm)),
    ]
    return pl.pallas_call(
        functools.partial(_mixer_kernel, ts=ts),
        name="mixer",
        grid=grid,
        in_specs=in_specs,
        out_specs=pl.BlockSpec((None, ts, dm), lambda b, j: (b, j, 0)),
        out_shape=jax.ShapeDtypeStruct((bsz, seq, dm), F32),
        scratch_shapes=[
            pltpu.VMEM((RET_QK, RET_WIDTH), F32),
            pltpu.VMEM((CONV_HIST + ts, CONV_CH), F32),
            pltpu.VMEM((ts, CONV_CH), F32),
            pltpu.VMEM((SUBLANES, ts + CONV_HIST - SUBLANES, CONV_CH), F32),
        ],
        compiler_params=pltpu.CompilerParams(
            dimension_semantics=("arbitrary", "arbitrary"), vmem_limit_bytes=VMEM_LIMIT_BYTES),
    )(x, row(g), w_in.astype(BF16), cos, sin, qd, kd, intra, cd, bd, hm, row(gn_g), dw_w.astype(F32),
      row(dw_b), row(ln_g), row(ln_b), w_ret.astype(BF16), w_conv.astype(BF16), row(b_conv),
      w_out.astype(BF16))


def _zip_steps(*steppers):
    results = [None] * len(steppers)
    live = dict(enumerate(steppers))
    while live:
        for idx in list(live):
            try:
                next(live[idx])
            except StopIteration as done:
                results[idx] = done.value
                del live[idx]
        if live:
            yield
    return results


def _interleave(*steppers):
    zipped = _zip_steps(*steppers)
    try:
        while True:
            next(zipped)
    except StopIteration as done:
        return done.value


def _top16(s, row_id, break_ties, want_rank=True):
    cur = s
    rank = jnp.full(s.shape, NOT_SELECTED, F32) if want_rank else None
    vals = []
    for r in range(PEER_TOPK):
        m = jnp.max(cur, axis=0, keepdims=True)
        if break_ties:
            first = jnp.min(jnp.where(cur == m, row_id, 1e9), axis=0, keepdims=True)
            hit = row_id == first
        else:
            hit = cur == m
        if want_rank:
            rank = jnp.where(hit, float(r), rank)
        cur = jnp.where(hit, -jnp.inf, cur)
        vals.append(m)
        yield
    return jnp.concatenate(vals, axis=0), rank


def _merge_exchange_network(n):
    pairs = []
    t = n.bit_length() - 1
    p = 1 << (t - 1)
    while p > 0:
        q, r, d = 1 << (t - 1), 0, p
        while d > 0:
            pairs += [(i, i + d) for i in range(n - d) if (i & p) == r]
            d, q, r = q - p, q >> 1, p
        p >>= 1
    return pairs


def _top16_values(rows):
    k = PEER_TOPK
    n_real = len(rows)
    assert n_real <= k
    x = list(rows) + [None] * (k - n_real)
    for n, (i, j) in enumerate(_merge_exchange_network(k)):
        if x[i] is None:
            x[i], x[j] = x[j], None
        elif x[j] is not None:
            x[i], x[j] = jnp.maximum(x[i], x[j]), jnp.minimum(x[i], x[j])
        if n % 8 == 7:
            yield
    x = [jnp.full((SUBLANES, LANES), -jnp.inf, F32) if xi is None else xi for xi in x]
    shift = SUBLANES // 2
    while shift >= 1:
        x = [jnp.maximum(x[i], pltpu.roll(x[k - 1 - i], shift, 0)) for i in range(k)]
        yield
        stride = k // 2
        while stride >= 1:
            for i in range(k):
                if (i & stride) == 0:
                    x[i], x[i + stride] = jnp.maximum(x[i], x[i + stride]), jnp.minimum(x[i], x[i + stride])
            stride //= 2
            yield
        shift //= 2
    dup = jnp.zeros((SUBLANES, LANES), F32)
    for i in range(k - 1):
        dup = jnp.where(x[i] == x[i + 1], 1.0, dup)
    row = lax.broadcasted_iota(jnp.int32, (SUBLANES, LANES), 0)
    halves = []
    for h0 in (0, SUBLANES):
        v = x[h0]
        for i in range(1, SUBLANES):
            v = jnp.where(row == i, x[h0 + i], v)
        halves.append(v)
    return jnp.concatenate(halves, axis=0), dup


def _staircase(v1, v2, e1x, e2x, break_ties):
    k = PEER_TOPK
    row = lax.broadcasted_iota(jnp.int32, (SUBLANES, LANES), 0).astype(F32)
    ninf = jnp.full((SUBLANES, LANES), -jnp.inf, F32)
    cand, pos, wgt = [], [], []
    for b0 in (0, SUBLANES):
        cand.append(v1[0:1] + v2[b0:b0 + SUBLANES]); pos.append(row + float(b0))
        wgt.append(e1x[0:1] * e2x[b0:b0 + SUBLANES])
    cand.append(v1[1:2] + v2[0:SUBLANES]); pos.append(row + float(k)); wgt.append(e1x[1:2] * e2x[0:SUBLANES])
    cand.append(v1[SUBLANES:k] + v2[0:1]); pos.append((row + float(SUBLANES)) * float(k))
    wgt.append(e1x[SUBLANES:k] * e2x[0:1])
    n_col = 5
    for b in range(n_col):
        a_max = float(k // (b + 1) - 1)
        ok = (row >= 2.0) & (row <= a_max)
        cand.append(jnp.where(ok, v1[0:SUBLANES] + v2[b:b + 1], ninf))
        pos.append(jnp.where(ok, row * float(k) + float(b), -1.0))
        wgt.append(e1x[0:SUBLANES] * e2x[b:b + 1])
    ng = len(cand)
    sel = [jnp.zeros((SUBLANES, LANES), F32) for _ in range(ng)]
    if not break_ties:
        best, _ = yield from _top16_values(cand)
        sel = [jnp.where(c >= best[k - 1:k], 1.0, 0.0) for c in cand]
    for _ in range(k if break_ties else 0):
        m = cand[0]
        for g in range(1, ng):
            m = jnp.maximum(m, cand[g])
        m = jnp.max(m, axis=0, keepdims=True)
        fp = jnp.where(cand[0] == m, pos[0], 1e9)
        for g in range(1, ng):
            fp = jnp.minimum(fp, jnp.where(cand[g] == m, pos[g], 1e9))
        fp = jnp.min(fp, axis=0, keepdims=True)
        for g in range(ng):
            hit = pos[g] == fp
            sel[g] = jnp.where(hit, 1.0, sel[g])
            cand[g] = jnp.where(hit, -jnp.inf, cand[g])
        yield
    b0cnt = jnp.sum(sel[0] + sel[1], axis=0, keepdims=True)
    b1cnt = jnp.sum(sel[2], axis=0, keepdims=True)
    low = sel[4]
    for g in range(5, ng):
        low = low + sel[g]
    low = low + jnp.where(row == 0.0, b0cnt, 0.0) + jnp.where(row == 1.0, b1cnt, 0.0)
    counts = jnp.concatenate([low, sel[3]], axis=0)
    zt = sel[0] * wgt[0]
    nsel = sel[0]
    for g in range(1, ng):
        zt = zt + sel[g] * wgt[g]
        nsel = nsel + sel[g]
    return counts, jnp.sum(zt, axis=0, keepdims=True), jnp.sum(nsel, axis=0, keepdims=True)


def _peer_kernel(x_ref, g_ref, wq_ref, keys_ref, u0_ref, u_ref, vt_ref, gfin_ref, o_ref,
                 hn_ref, s_ref, vbuf_ref, w2_ref, thr_ref, w1_ref, st_ref, at_ref, acc_ref,
                 *, tt, eb, final_norm):
    j = pl.program_id(1)
    n_blk = pl.num_programs(1) - 1
    nch = tt // PEER_CHUNK
    lpc = PEER_CHUNK // LANES
    qd = 2 * PEER_HALF

    @pl.when(j == 0)
    def _routing():
        x = x_ref[...]
        hn = x * lax.rsqrt(jnp.mean(x * x, axis=-1, keepdims=True) + RMS_EPS) * g_ref[...]
        for c in range(nch):
            hn_t = hn[c * PEER_CHUNK:(c + 1) * PEER_CHUNK, :].T
            hq_c = hn_t.astype(BF16)
            hn_c = (hn_t * INV_SQRT2).astype(BF16)
            hn_ref[c] = hn_c
            acc_ref[c] = jnp.zeros(acc_ref.shape[1:], F32)
            at_ref[1, c] = jnp.zeros(at_ref.shape[2:], BF16)
            st_ref[0, c] = jnp.dot(u0_ref[...], hn_c, preferred_element_type=F32)
            for hh in range(PEER_HEADS):
                qt = jnp.dot(wq_ref[hh * qd:(hh + 1) * qd, :], hq_c, preferred_element_type=F32).astype(BF16)
                for p in range(2):
                    sc = jnp.dot(keys_ref[2 * hh + p], qt[p * PEER_HALF:(p + 1) * PEER_HALF, :],
                                 preferred_element_type=F32)
                    for l in range(lpc):
                        s_ref[2 * hh + p, c * lpc + l] = sc[:, l * LANES:(l + 1) * LANES]

        row_id = lax.broadcasted_iota(jnp.int32, (N_KEYS, LANES), 0).astype(F32)

        n_route = PEER_HEADS * nch * lpc
        topk = float(PEER_TOPK)

        def scores(blk):
            hh = blk // (nch * lpc)
            lb = blk % (nch * lpc)
            return hh, lb, s_ref[2 * hh, lb], s_ref[2 * hh + 1, lb]

        def dense_out(hh, lb, s1, s2, v1, v2, z, thr):
            thr_ref[hh, lb] = thr
            w1_ref[hh, lb] = jnp.exp(s1 - v1[0:1]) * (INV_SQRT2 / z)
            w2_ref[hh, lb] = jnp.exp(s2 - v2[0:1])

        def fast_a(blk, slot):
            hh, lb, s1, s2 = scores(blk)
            groups = lambda s: [s[i * SUBLANES:(i + 1) * SUBLANES, :] for i in range(N_KEYS // SUBLANES)]
            (v1, dup1), (v2, dup2) = yield from _zip_steps(_top16_values(groups(s1)), _top16_values(groups(s2)))
            last = slice(PEER_TOPK - 1, PEER_TOPK)
            in_top = jnp.where(s1 >= v1[last], 1.0, 0.0) + jnp.where(s2 >= v2[last], 1.0, 0.0)
            vbuf_ref[slot, 0] = v1
            vbuf_ref[slot, 1] = v2
            vbuf_ref[slot, 2, 0:1, :] = (jnp.sum(in_top, axis=0, keepdims=True)
                                         + jnp.max(dup1 + dup2, axis=0, keepdims=True))

        def fast_b(blk, slot):
            hh, lb, s1, s2 = scores(blk)
            v1 = vbuf_ref[slot, 0]
            v2 = vbuf_ref[slot, 1]
            counts, z, nsel = yield from _staircase(v1, v2, jnp.exp(v1 - v1[0:1]), jnp.exp(v2 - v2[0:1]), False)
            tval = jnp.full((PEER_TOPK, LANES), jnp.inf, F32)
            for b in range(PEER_TOPK):
                tval = jnp.where(counts == float(b + 1), v2[b:b + 1], tval)
            thr = jnp.full((N_KEYS, LANES), jnp.inf, F32)
            for a in range(SUBLANES):
                thr = jnp.where(s1 == v1[a:a + 1], tval[a:a + 1], thr)
            lo = jnp.min(jnp.where(counts[SUBLANES:] > 0.0, v1[SUBLANES:], jnp.inf), axis=0, keepdims=True)
            thr = jnp.where(s1 < v1[SUBLANES - 1:SUBLANES], jnp.where(s1 >= lo, v2[0:1], thr), thr)
            dense_out(hh, lb, s1, s2, v1, v2, z, thr)
            return vbuf_ref[slot, 2, 0:1, :] + nsel

        def exact(blk):
            hh, lb, s1, s2 = scores(blk)
            (v1, rank1), (v2, rank2) = _interleave(_top16(s1, row_id, True), _top16(s2, row_id, True))
            (counts, z, _), = _interleave(
                _staircase(v1, v2, jnp.exp(v1 - v1[0:1]), jnp.exp(v2 - v2[0:1]), True))
            bcnt = jnp.zeros((N_KEYS, LANES), F32)
            for a in range(PEER_TOPK):
                bcnt = jnp.where(rank1 == float(a), counts[a:a + 1], bcnt)
            dense_out(hh, lb, s1, s2, v1, v2, z, jnp.where(bcnt > 0.0, 1.0 - bcnt, jnp.inf))
            s_ref[2 * hh + 1, lb] = -rank2

        _interleave(fast_a(0, 0))

        def route(i, carry):
            slot = i % 2
            _, picked = _interleave(fast_a(jnp.minimum(i + 1, n_route - 1), 1 - slot), fast_b(i, slot))
            tied = jnp.max(jnp.where(picked == 3.0 * topk, 0.0, 1.0))

            @pl.when(tied > 0.0)
            def _():
                exact(i)

            return carry

        lax.fori_loop(0, n_route, route, 0)

    def steady(cur, nxt):
        def chunk(c, carry):
            acc_ref[c] += jnp.dot(vt_ref[...], at_ref[nxt, c], preferred_element_type=F32)
            for e0 in range(0, eb, BUILD_E1):
                for l in range(lpc):
                    lb = c * lpc + l
                    cols = slice(l * LANES, (l + 1) * LANES)
                    for g in range(N_KEYS // BUILD_ROWS):
                        grow = slice(g * BUILD_ROWS, (g + 1) * BUILD_ROWS)
                        gates = [jnp.zeros((BUILD_ROWS, LANES), F32) for _ in range(BUILD_E1)]
                        for hh in range(PEER_HEADS):
                            key = s_ref[2 * hh + 1, lb, grow, :]
                            w2 = w2_ref[hh, lb, grow, :]
                            for k in range(BUILD_E1):
                                e1 = j * eb + e0 + k
                                thr = thr_ref[hh, lb, pl.ds(e1, 1), :]
                                w1 = w1_ref[hh, lb, pl.ds(e1, 1), :]
                                gates[k] = gates[k] + jnp.where(key >= thr, w2, 0.0) * w1
                        for k in range(BUILD_E1):
                            rows = slice((e0 + k) * N_KEYS + g * BUILD_ROWS, (e0 + k) * N_KEYS + (g + 1) * BUILD_ROWS)
                            xs = st_ref[cur, c, rows, cols]
                            ge = xs * (1.0 + lax.erf(xs))
                            at_ref[cur, c, rows, cols] = (ge * gates[k]).astype(BF16)
            st_ref[nxt, c] = jnp.dot(u_ref[...], hn_ref[c], preferred_element_type=F32)
            return carry

        lax.fori_loop(0, nch, chunk, 0)

    @pl.when((j < n_blk) & (j % 2 == 0))
    def _even():
        steady(0, 1)

    @pl.when((j < n_blk) & (j % 2 == 1))
    def _odd():
        steady(1, 0)

    @pl.when(j == n_blk)
    def _finish():
        last = (N_KEYS // eb - 1) % 2
        for c in range(nch):
            acc = acc_ref[c] + jnp.dot(vt_ref[...], at_ref[last, c], preferred_element_type=F32)
            rows = slice(c * PEER_CHUNK, (c + 1) * PEER_CHUNK)
            y = acc.T + x_ref[rows, :]
            if final_norm:
                y = y * lax.rsqrt(jnp.mean(y * y, axis=-1, keepdims=True) + RMS_EPS) * gfin_ref[...]
            o_ref[rows, :] = y


def _peer_tiles(tokens):
    for tt in (512, 256):
        if tokens % tt == 0:
            return tt, 8
    raise ValueError(f"token count {tokens} must be a multiple of {PEER_CHUNK}")


def _peer(x, g, w_query, sub_keys, u_tab, v_tab, g_final, final_norm):
    tokens, dm = x.shape
    tt, eb = _peer_tiles(tokens)
    nlb = tt // LANES
    nch = tt // PEER_CHUNK
    eblk = eb * N_KEYS
    n_exp = u_tab.shape[0]
    assert n_exp == N_KEYS * N_KEYS and N_KEYS % eb == 0
    assert sub_keys.shape == (PEER_HEADS, 2, N_KEYS, PEER_HALF)
    n_blk = N_KEYS // eb
    wq_t = w_query.T.astype(BF16)
    keys = sub_keys.reshape(PEER_HEADS * 2, N_KEYS, PEER_HALF).astype(BF16)
    u_bf = u_tab.astype(BF16)
    vt_bf = v_tab.T.astype(BF16)
    row = lambda a: a.reshape(1, -1).astype(F32)
    grid = (tokens // tt, n_blk + 1)
    return pl.pallas_call(
        functools.partial(_peer_kernel, tt=tt, eb=eb, final_norm=final_norm),
        name="peer",
        grid=grid,
        in_specs=[
            pl.BlockSpec((tt, dm), lambda i, j: (i, 0)),
            _const_spec((1, dm)),
            _const_spec(wq_t.shape),
            _const_spec(keys.shape),
            _const_spec((eblk, dm)),
            pl.BlockSpec((eblk, dm), lambda i, j: (jnp.minimum(j + 1, n_blk - 1), 0)),
            pl.BlockSpec((dm, eblk), lambda i, j: (0, jnp.clip(j - 1, 0, n_blk - 1))),
            _const_spec((1, dm)),
        ],
        out_specs=pl.BlockSpec((tt, dm), lambda i, j: (i, 0)),
        out_shape=jax.ShapeDtypeStruct((tokens, dm), F32),
        scratch_shapes=[
            pltpu.VMEM((nch, dm, PEER_CHUNK), BF16),
            pltpu.VMEM((2 * PEER_HEADS, nlb, N_KEYS, LANES), F32),
            pltpu.VMEM((2, 3, PEER_TOPK, LANES), F32),
            pltpu.VMEM((PEER_HEADS, nlb, N_KEYS, LANES), F32),
            pltpu.VMEM((PEER_HEADS, nlb, N_KEYS, LANES), F32),
            pltpu.VMEM((PEER_HEADS, nlb, N_KEYS, LANES), F32),
            pltpu.VMEM((2, nch, eblk, PEER_CHUNK), F32),
            pltpu.VMEM((2, nch, eblk, PEER_CHUNK), BF16),
            pltpu.VMEM((nch, dm, PEER_CHUNK), F32),
        ],
        compiler_params=pltpu.CompilerParams(
            dimension_semantics=("arbitrary", "arbitrary"), vmem_limit_bytes=VMEM_LIMIT_BYTES),
    )(x, row(g), wq_t, keys, u_bf, u_bf, vt_bf, row(g_final))


def kernel(x, rms_mix_g, w_in, ret_gn_g, conv_dw_w, conv_dw_b, conv_ln_g, conv_ln_b, w_ret_proj, w_conv_proj,
           b_conv_proj, w_out, rms_ffn_g, w_query, peer_sub_keys, peer_u, peer_v, rms_final_g):
    bsz, seq, dm = x.shape
    depth = w_in.shape[0]
    for l in range(depth):
        x = _mixer(x, rms_mix_g[l], w_in[l], ret_gn_g[l], conv_dw_w[l], conv_dw_b[l], conv_ln_g[l],
                   conv_ln_b[l], w_ret_proj[l], w_conv_proj[l], b_conv_proj[l], w_out[l])
        x = _peer(x.reshape(bsz * seq, dm), rms_ffn_g[l], w_query[l], peer_sub_keys[l], peer_u[l], peer_v[l],
                  rms_final_g, final_norm=(l == depth - 1)).reshape(bsz, seq, dm)
    return x
```

```python
import functools

import numpy as np
import jax
import jax.numpy as jnp
from jax import lax
from jax.experimental import pallas as pl
from jax.experimental.pallas import tpu as pltpu

F32 = jnp.float32
BF16 = jnp.bfloat16

RET_HEADS = 4
RET_QK_DIM = 64
RET_V_DIM = 128
RET_QK = RET_HEADS * RET_QK_DIM
RET_WIDTH = RET_HEADS * RET_V_DIM
ROPE_BASE = 10000.0
CONV_CH = 512
CONV_K = 31
PEER_HEADS = 8
PEER_HALF = 128
N_KEYS = 128
PEER_TOPK = 16
RMS_EPS = 1e-6
LN_EPS = 1e-5
INV_SQRT2 = 0.7071067811865476

LANES = 128
SUBLANES = 8
VMEM_LIMIT_BYTES = 56 * 1024 * 1024
CONV_HIST = 32
CONV_ROWS = 32
PEER_CHUNK = 256
BUILD_ROWS = 32
BUILD_E1 = 4
NOT_SELECTED = 127.0


def _const_spec(shape):
    nd = len(shape)
    return pl.BlockSpec(shape, lambda *_: (0,) * nd, pipeline_mode=pl.Buffered(1))


def _swap_halves(t):
    half = RET_QK_DIM // 2
    parts = []
    for c in range(t.shape[1] // LANES):
        tc = t[:, c * LANES:(c + 1) * LANES]
        lane = lax.broadcasted_iota(jnp.int32, tc.shape, 1)
        first = (lane % RET_QK_DIM) < half
        nxt = pltpu.roll(tc, LANES - half, 1)
        prv = pltpu.roll(tc, half, 1)
        parts.append(jnp.where(first, nxt, prv))
    return jnp.concatenate(parts, axis=1)


def _mixer_kernel(x_ref, g_ref, win_ref, cos_ref, sin_ref, qd_ref, kd_ref, intra_ref, cd_ref, bd_ref,
                  hm_ref, gn_ref, dww_ref, dwb_ref, lng_ref, lnb_ref, wret_ref, wconv_ref, bconv_ref,
                  wout_ref, o_ref, state_ref, cbuf_ref, conv_ref, shift_ref, *, ts):
    @pl.when(pl.program_id(1) == 0)
    def _():
        state_ref[...] = jnp.zeros_like(state_ref)
        cbuf_ref[0:CONV_HIST, :] = jnp.zeros((CONV_HIST, CONV_CH), F32)

    x = x_ref[...]
    h = (x * lax.rsqrt(jnp.mean(x * x, axis=-1, keepdims=True) + RMS_EPS) * g_ref[...]).astype(BF16)

    c0 = 2 * RET_QK + RET_WIDTH
    qkv = jnp.dot(h, win_ref[:, 0:c0], preferred_element_type=F32)
    q = qkv[:, 0:RET_QK]
    k = qkv[:, RET_QK:2 * RET_QK]
    vb = qkv[:, 2 * RET_QK:c0].astype(BF16)
    cos = cos_ref[...]
    sin = sin_ref[...]
    qr = q * cos + _swap_halves(q) * sin
    kr = (k * cos + _swap_halves(k) * sin) * (RET_QK_DIM ** -0.5)
    qb = qr.astype(BF16)
    kbt = kr.T.astype(BF16)
    inner = []
    for hh in range(RET_HEADS):
        qm = (qr * hm_ref[hh:hh + 1, :]).astype(BF16)
        sc = jnp.dot(qm, kbt, preferred_element_type=F32)
        sc = sc * intra_ref[hh]
        inner.append(jnp.dot(sc.astype(BF16), vb[:, hh * RET_V_DIM:(hh + 1) * RET_V_DIM],
                             preferred_element_type=F32))
    st = state_ref[...]
    cross = jnp.dot(qb, st.astype(BF16), preferred_element_type=F32) * qd_ref[...]
    o = jnp.concatenate(inner, axis=1) + cross
    kv = lax.dot_general((kr * kd_ref[...]).astype(BF16), vb, (((0,), (0,)), ((), ())),
                         preferred_element_type=F32)
    state_ref[...] = st * cd_ref[...] + kv * bd_ref[...]

    normed = []
    for hh in range(RET_HEADS):
        oh = o[:, hh * RET_V_DIM:(hh + 1) * RET_V_DIM]
        d = oh - jnp.mean(oh, axis=-1, keepdims=True)
        normed.append(d * lax.rsqrt(jnp.mean(d * d, axis=-1, keepdims=True) + LN_EPS))
    on = jnp.concatenate(normed, axis=1) * gn_ref[...]

    c1 = c0 + RET_WIDTH + 2 * CONV_CH
    p2 = jnp.dot(h, win_ref[:, c0:c1], preferred_element_type=F32)
    g_ret = p2[:, 0:RET_WIDTH]
    yret = (g_ret * jax.nn.sigmoid(g_ret) * on).astype(BF16)
    a_out = jnp.dot(yret, wret_ref[...], preferred_element_type=F32)

    u_val = p2[:, RET_WIDTH:RET_WIDTH + CONV_CH]
    u_gate = p2[:, RET_WIDTH + CONV_CH:RET_WIDTH + 2 * CONV_CH]
    cbuf_ref[CONV_HIST:CONV_HIST + ts, :] = u_val * jax.nn.sigmoid(u_gate)
    base = CONV_HIST - (CONV_K - 1)
    groups = [[o for o in range(base, base + CONV_K) if o % SUBLANES == r] for r in range(SUBLANES)]
    for r in range(1, SUBLANES):
        offs = groups[r]
        n_rows = ts + offs[-1] - offs[0]
        shift_ref[r, 0:n_rows, :] = cbuf_ref[pl.ds(offs[0], n_rows), :]
    tiles = (CONV_ROWS // SUBLANES, SUBLANES, CONV_CH)
    for rc in range(ts // CONV_ROWS):
        acc = jnp.broadcast_to(dwb_ref[...][None], tiles)
        for r in range(SUBLANES):
            for o in groups[r]:
                if r == 0:
                    tap = cbuf_ref[pl.ds(rc * CONV_ROWS + o, CONV_ROWS), :]
                else:
                    tap = shift_ref[r, pl.ds(rc * CONV_ROWS + o - groups[r][0], CONV_ROWS), :]
                acc = acc + dww_ref[o - base][None] * tap.reshape(tiles)
        conv_ref[rc * CONV_ROWS:(rc + 1) * CONV_ROWS, :] = acc.reshape(CONV_ROWS, CONV_CH)
    cbuf_ref[0:CONV_HIST, :] = cbuf_ref[ts:ts + CONV_HIST, :]
    cv = conv_ref[...]
    d = cv - jnp.mean(cv, axis=-1, keepdims=True)
    ln = d * lax.rsqrt(jnp.mean(d * d, axis=-1, keepdims=True) + LN_EPS) * lng_ref[...] + lnb_ref[...]
    yconv = (ln * jax.nn.sigmoid(ln)).astype(BF16)
    b_out = jnp.dot(yconv, wconv_ref[...], preferred_element_type=F32) + bconv_ref[...]

    p3 = jnp.dot(h, win_ref[:, c1:], preferred_element_type=F32)
    dm = x.shape[1]
    mixed = jax.nn.sigmoid(p3[:, 0:dm]) * a_out + jax.nn.sigmoid(p3[:, dm:2 * dm]) * b_out
    o_ref[...] = x + jnp.dot(mixed.astype(BF16), wout_ref[...], preferred_element_type=F32)


def _mixer_tables(seq, ts):
    half = RET_QK_DIM // 2
    inv = 1.0 / (ROPE_BASE ** (np.arange(half, dtype=np.float64) / half))
    ang = np.arange(seq, dtype=np.float64)[:, None] * inv[None, :]
    cos = np.tile(np.concatenate([np.cos(ang), np.cos(ang)], axis=1), (1, RET_HEADS))
    sin = np.tile(np.concatenate([-np.sin(ang), np.sin(ang)], axis=1), (1, RET_HEADS))
    log_gamma = np.log1p(-(2.0 ** (-5.0 - np.arange(RET_HEADS, dtype=np.float64))))
    n = np.arange(ts, dtype=np.float64)
    diff = n[:, None] - n[None, :]
    intra = np.where(diff[None] >= 0, np.exp(log_gamma[:, None, None] * np.maximum(diff, 0.0)[None]), 0.0)
    q_decay = np.exp(log_gamma[:, None] * (n + 1.0)[None])
    k_decay = np.exp(log_gamma[:, None] * (ts - 1.0 - n)[None])
    chunk_decay = np.exp(log_gamma * ts)
    qd = np.repeat(q_decay.T, RET_V_DIM, axis=1)
    kd = np.repeat(k_decay.T, RET_QK_DIM, axis=1)
    cd = np.repeat(chunk_decay, RET_V_DIM)[None, :]
    row_head = np.arange(RET_QK) // RET_QK_DIM
    col_head = np.arange(RET_WIDTH) // RET_V_DIM
    bd = (row_head[:, None] == col_head[None, :]).astype(np.float64)
    hm = (np.arange(RET_HEADS)[:, None] == row_head[None, :]).astype(np.float64)
    as32 = lambda a: jnp.asarray(a, dtype=F32)
    return tuple(as32(a) for a in (cos, sin, qd, kd, intra, cd, bd, hm))


def _mixer_tile(seq):
    for ts in (256, 128, 64):
        if seq % ts == 0:
            return ts
    raise ValueError(f"sequence length {seq} must be a multiple of {CONV_ROWS}")


def _mixer(x, g, w_in, gn_g, dw_w, dw_b, ln_g, ln_b, w_ret, w_conv, b_conv, w_out):
    bsz, seq, dm = x.shape
    ts = _mixer_tile(seq)
    cos, sin, qd, kd, intra, cd, bd, hm = _mixer_tables(seq, ts)
    row = lambda a: a.reshape(1, -1).astype(F32)
    in_cols = w_in.shape[1]
    grid = (bsz, seq // ts)
    in_specs = [
        pl.BlockSpec((None, ts, dm), lambda b, j: (b, j, 0)),
        _const_spec((1, dm)),
        _const_spec((dm, in_cols)),
        pl.BlockSpec((ts, RET_QK), lambda b, j: (j, 0)),
        pl.BlockSpec((ts, RET_QK), lambda b, j: (j, 0)),
        _const_spec((ts, RET_WIDTH)),
        _const_spec((ts, RET_QK)),
        _const_spec((RET_HEADS, ts, ts)),
        _const_spec((1, RET_WIDTH)),
        _const_spec((RET_QK, RET_WIDTH)),
        _const_spec((RET_HEADS, RET_QK)),
        _const_spec((1, RET_WIDTH)),
        _const_spec((CONV_K, SUBLANES, CONV_CH)),
        _const_spec((SUBLANES, CONV_CH)),
        _const_spec((1, CONV_CH)),
        _const_spec((1, CONV_CH)),
        _const_spec((RET_WIDTH, dm)),
        _const_spec((CONV_CH, dm)),
        _const_spec((1, dm)),
        _const_spec((dm, dm)),
    ]
    return pl.pallas_call(
        functools.partial(_mixer_kernel, ts=ts),
        name="mixer",
        grid=grid,
        in_specs=in_specs,
        out_specs=pl.BlockSpec((None, ts, dm), lambda b, j: (b, j, 0)),
        out_shape=jax.ShapeDtypeStruct((bsz, seq, dm), F32),
        scratch_shapes=[
            pltpu.VMEM((RET_QK, RET_WIDTH), F32),
            pltpu.VMEM((CONV_HIST + ts, CONV_CH), F32),
            pltpu.VMEM((ts, CONV_CH), F32),
            pltpu.VMEM((SUBLANES, ts + CONV_HIST - SUBLANES, CONV_CH), F32),
        ],
        compiler_params=pltpu.CompilerParams(
            dimension_semantics=("arbitrary", "arbitrary"), vmem_limit_bytes=VMEM_LIMIT_BYTES),
    )(x, row(g), w_in.astype(BF16), cos, sin, qd, kd, intra, cd, bd, hm, row(gn_g),
      jnp.broadcast_to(dw_w.astype(F32)[:, None, :], (CONV_K, SUBLANES, CONV_CH)),
      jnp.broadcast_to(row(dw_b), (SUBLANES, CONV_CH)), row(ln_g), row(ln_b), w_ret.astype(BF16), w_conv.astype(BF16), row(b_conv),
      w_out.astype(BF16))


def _zip_steps(*steppers):
    results = [None] * len(steppers)
    live = dict(enumerate(steppers))
    while live:
        for idx in list(live):
            try:
                next(live[idx])
            except StopIteration as done:
                results[idx] = done.value
                del live[idx]
        if live:
            yield
    return results


def _interleave(*steppers):
    zipped = _zip_steps(*steppers)
    try:
        while True:
            next(zipped)
    except StopIteration as done:
        return done.value


def _top16(s, row_id, break_ties, want_rank=True):
    cur = s
    rank = jnp.full(s.shape, NOT_SELECTED, F32) if want_rank else None
    vals = []
    for r in range(PEER_TOPK):
        m = jnp.max(cur, axis=0, keepdims=True)
        if break_ties:
            first = jnp.min(jnp.where(cur == m, row_id, 1e9), axis=0, keepdims=True)
            hit = row_id == first
        else:
            hit = cur == m
        if want_rank:
            rank = jnp.where(hit, float(r), rank)
        cur = jnp.where(hit, -jnp.inf, cur)
        vals.append(m)
        yield
    return jnp.concatenate(vals, axis=0), rank


def _merge_exchange_network(n):
    pairs = []
    t = n.bit_length() - 1
    p = 1 << (t - 1)
    while p > 0:
        q, r, d = 1 << (t - 1), 0, p
        while d > 0:
            pairs += [(i, i + d) for i in range(n - d) if (i & p) == r]
            d, q, r = q - p, q >> 1, p
        p >>= 1
    return pairs


def _top16_values(rows):
    k = PEER_TOPK
    n_real = len(rows)
    assert n_real <= k
    x = list(rows) + [None] * (k - n_real)
    for n, (i, j) in enumerate(_merge_exchange_network(k)):
        if x[i] is None:
            x[i], x[j] = x[j], None
        elif x[j] is not None:
            x[i], x[j] = jnp.maximum(x[i], x[j]), jnp.minimum(x[i], x[j])
        if n % 8 == 7:
            yield
    x = [jnp.full((SUBLANES, LANES), -jnp.inf, F32) if xi is None else xi for xi in x]
    shift = SUBLANES // 2
    while shift >= 1:
        x = [jnp.maximum(x[i], pltpu.roll(x[k - 1 - i], shift, 0)) for i in range(k)]
        yield
        stride = k // 2
        while stride >= 1:
            for i in range(k):
                if (i & stride) == 0:
                    x[i], x[i + stride] = jnp.maximum(x[i], x[i + stride]), jnp.minimum(x[i], x[i + stride])
            stride //= 2
            yield
        shift //= 2
    dup = jnp.zeros((SUBLANES, LANES), F32)
    for i in range(k - 1):
        dup = jnp.where(x[i] == x[i + 1], 1.0, dup)
    row = lax.broadcasted_iota(jnp.int32, (SUBLANES, LANES), 0)
    halves = []
    for h0 in (0, SUBLANES):
        v = x[h0]
        for i in range(1, SUBLANES):
            v = jnp.where(row == i, x[h0 + i], v)
        halves.append(v)
    return jnp.concatenate(halves, axis=0), dup


def _staircase(v1, v2, e1x, e2x, break_ties):
    k = PEER_TOPK
    row = lax.broadcasted_iota(jnp.int32, (SUBLANES, LANES), 0).astype(F32)
    ninf = jnp.full((SUBLANES, LANES), -jnp.inf, F32)
    cand, pos, wgt = [], [], []
    for b0 in (0, SUBLANES):
        cand.append(v1[0:1] + v2[b0:b0 + SUBLANES]); pos.append(row + float(b0))
        wgt.append(e1x[0:1] * e2x[b0:b0 + SUBLANES])
    cand.append(v1[1:2] + v2[0:SUBLANES]); pos.append(row + float(k)); wgt.append(e1x[1:2] * e2x[0:SUBLANES])
    cand.append(v1[SUBLANES:k] + v2[0:1]); pos.append((row + float(SUBLANES)) * float(k))
    wgt.append(e1x[SUBLANES:k] * e2x[0:1])
    n_col = 5
    for b in range(n_col):
        a_max = float(k // (b + 1) - 1)
        ok = (row >= 2.0) & (row <= a_max)
        cand.append(jnp.where(ok, v1[0:SUBLANES] + v2[b:b + 1], ninf))
        pos.append(jnp.where(ok, row * float(k) + float(b), -1.0))
        wgt.append(e1x[0:SUBLANES] * e2x[b:b + 1])
    ng = len(cand)
    sel = [jnp.zeros((SUBLANES, LANES), F32) for _ in range(ng)]
    if not break_ties:
        best, _ = yield from _top16_values(cand)
        sel = [jnp.where(c >= best[k - 1:k], 1.0, 0.0) for c in cand]
    for _ in range(k if break_ties else 0):
        m = cand[0]
        for g in range(1, ng):
            m = jnp.maximum(m, cand[g])
        m = jnp.max(m, axis=0, keepdims=True)
        fp = jnp.where(cand[0] == m, pos[0], 1e9)
        for g in range(1, ng):
            fp = jnp.minimum(fp, jnp.where(cand[g] == m, pos[g], 1e9))
        fp = jnp.min(fp, axis=0, keepdims=True)
        for g in range(ng):
            hit = pos[g] == fp
            sel[g] = jnp.where(hit, 1.0, sel[g])
            cand[g] = jnp.where(hit, -jnp.inf, cand[g])
        yield
    b0cnt = jnp.sum(sel[0] + sel[1], axis=0, keepdims=True)
    b1cnt = jnp.sum(sel[2], axis=0, keepdims=True)
    low = sel[4]
    for g in range(5, ng):
        low = low + sel[g]
    low = low + jnp.where(row == 0.0, b0cnt, 0.0) + jnp.where(row == 1.0, b1cnt, 0.0)
    counts = jnp.concatenate([low, sel[3]], axis=0)
    zt = sel[0] * wgt[0]
    nsel = sel[0]
    for g in range(1, ng):
        zt = zt + sel[g] * wgt[g]
        nsel = nsel + sel[g]
    return counts, jnp.sum(zt, axis=0, keepdims=True), jnp.sum(nsel, axis=0, keepdims=True)


def _peer_kernel(x_ref, g_ref, wq_ref, keys_ref, u0_ref, u_ref, vt_ref, gfin_ref, o_ref,
                 hn_ref, s_ref, vbuf_ref, w2_ref, thr_ref, w1_ref, st_ref, at_ref, acc_ref,
                 *, tt, eb, final_norm):
    j = pl.program_id(1)
    n_blk = pl.num_programs(1) - 1
    nch = tt // PEER_CHUNK
    lpc = PEER_CHUNK // LANES
    qd = 2 * PEER_HALF

    @pl.when(j == 0)
    def _routing():
        x = x_ref[...]
        hn = x * lax.rsqrt(jnp.mean(x * x, axis=-1, keepdims=True) + RMS_EPS) * g_ref[...]
        for c in range(nch):
            hn_t = hn[c * PEER_CHUNK:(c + 1) * PEER_CHUNK, :].T
            hq_c = hn_t.astype(BF16)
            hn_c = (hn_t * INV_SQRT2).astype(BF16)
            hn_ref[c] = hn_c
            acc_ref[c] = jnp.zeros(acc_ref.shape[1:], F32)
            at_ref[1, c] = jnp.zeros(at_ref.shape[2:], BF16)
            st_ref[0, c] = jnp.dot(u0_ref[...], hn_c, preferred_element_type=F32)
            for hh in range(PEER_HEADS):
                qt = jnp.dot(wq_ref[hh * qd:(hh + 1) * qd, :], hq_c, preferred_element_type=F32).astype(BF16)
                for p in range(2):
                    sc = jnp.dot(keys_ref[2 * hh + p], qt[p * PEER_HALF:(p + 1) * PEER_HALF, :],
                                 preferred_element_type=F32)
                    for l in range(lpc):
                        s_ref[2 * hh + p, c * lpc + l] = sc[:, l * LANES:(l + 1) * LANES]

        row_id = lax.broadcasted_iota(jnp.int32, (N_KEYS, LANES), 0).astype(F32)

        n_route = PEER_HEADS * nch * lpc
        topk = float(PEER_TOPK)

        def scores(blk):
            hh = blk // (nch * lpc)
            lb = blk % (nch * lpc)
            return hh, lb, s_ref[2 * hh, lb], s_ref[2 * hh + 1, lb]

        def dense_out(hh, lb, s1, s2, v1, v2, z, thr):
            thr_ref[hh, lb] = thr
            w1_ref[hh, lb] = jnp.exp(s1 - v1[0:1]) * (INV_SQRT2 / z)
            w2_ref[hh, lb] = jnp.exp(s2 - v2[0:1])

        def fast_a(blk, slot):
            hh, lb, s1, s2 = scores(blk)
            groups = lambda s: [s[i * SUBLANES:(i + 1) * SUBLANES, :] for i in range(N_KEYS // SUBLANES)]
            (v1, dup1), (v2, dup2) = yield from _zip_steps(_top16_values(groups(s1)), _top16_values(groups(s2)))
            last = slice(PEER_TOPK - 1, PEER_TOPK)
            in_top = jnp.where(s1 >= v1[last], 1.0, 0.0) + jnp.where(s2 >= v2[last], 1.0, 0.0)
            vbuf_ref[slot, 0] = v1
            vbuf_ref[slot, 1] = v2
            vbuf_ref[slot, 2, 0:1, :] = (jnp.sum(in_top, axis=0, keepdims=True)
                                         + jnp.max(dup1 + dup2, axis=0, keepdims=True))

        def fast_b(blk, slot):
            hh, lb, s1, s2 = scores(blk)
            v1 = vbuf_ref[slot, 0]
            v2 = vbuf_ref[slot, 1]
            counts, z, nsel = yield from _staircase(v1, v2, jnp.exp(v1 - v1[0:1]), jnp.exp(v2 - v2[0:1]), False)
            tval = jnp.full((PEER_TOPK, LANES), jnp.inf, F32)
            for b in range(PEER_TOPK):
                tval = jnp.where(counts == float(b + 1), v2[b:b + 1], tval)
            thr = jnp.full((N_KEYS, LANES), jnp.inf, F32)
            for a in range(SUBLANES):
                thr = jnp.where(s1 == v1[a:a + 1], tval[a:a + 1], thr)
            lo = jnp.min(jnp.where(counts[SUBLANES:] > 0.0, v1[SUBLANES:], jnp.inf), axis=0, keepdims=True)
            thr = jnp.where(s1 < v1[SUBLANES - 1:SUBLANES], jnp.where(s1 >= lo, v2[0:1], thr), thr)
            dense_out(hh, lb, s1, s2, v1, v2, z, thr)
            return vbuf_ref[slot, 2, 0:1, :] + nsel

        def exact(blk):
            hh, lb, s1, s2 = scores(blk)
            (v1, rank1), (v2, rank2) = _interleave(_top16(s1, row_id, True), _top16(s2, row_id, True))
            (counts, z, _), = _interleave(
                _staircase(v1, v2, jnp.exp(v1 - v1[0:1]), jnp.exp(v2 - v2[0:1]), True))
            bcnt = jnp.zeros((N_KEYS, LANES), F32)
            for a in range(PEER_TOPK):
                bcnt = jnp.where(rank1 == float(a), counts[a:a + 1], bcnt)
            dense_out(hh, lb, s1, s2, v1, v2, z, jnp.where(bcnt > 0.0, 1.0 - bcnt, jnp.inf))
            s_ref[2 * hh + 1, lb] = -rank2

        _interleave(fast_a(0, 0))

        def route(i, carry):
            slot = i % 2
            _, picked = _interleave(fast_a(jnp.minimum(i + 1, n_route - 1), 1 - slot), fast_b(i, slot))
            tied = jnp.max(jnp.where(picked == 3.0 * topk, 0.0, 1.0))

            @pl.when(tied > 0.0)
            def _():
                exact(i)

            return carry

        lax.fori_loop(0, n_route, route, 0)

    def steady(cur, nxt):
        def chunk(c, carry):
            acc_ref[c] += jnp.dot(vt_ref[...], at_ref[nxt, c], preferred_element_type=F32)
            for e0 in range(0, eb, BUILD_E1):
                for l in range(lpc):
                    lb = c * lpc + l
                    cols = slice(l * LANES, (l + 1) * LANES)
                    for g in range(N_KEYS // BUILD_ROWS):
                        grow = slice(g * BUILD_ROWS, (g + 1) * BUILD_ROWS)
                        gates = [jnp.zeros((BUILD_ROWS, LANES), F32) for _ in range(BUILD_E1)]
                        for hh in range(PEER_HEADS):
                            key = s_ref[2 * hh + 1, lb, grow, :]
                            w2 = w2_ref[hh, lb, grow, :]
                            for k in range(BUILD_E1):
                                e1 = j * eb + e0 + k
                                thr = thr_ref[hh, lb, pl.ds(e1, 1), :]
                                w1 = w1_ref[hh, lb, pl.ds(e1, 1), :]
                                gates[k] = gates[k] + jnp.where(key >= thr, w2, 0.0) * w1
                        for k in range(BUILD_E1):
                            rows = slice((e0 + k) * N_KEYS + g * BUILD_ROWS, (e0 + k) * N_KEYS + (g + 1) * BUILD_ROWS)
                            xs = st_ref[cur, c, rows, cols]
                            ge = xs * (1.0 + lax.erf(xs))
                            at_ref[cur, c, rows, cols] = (ge * gates[k]).astype(BF16)
            st_ref[nxt, c] = jnp.dot(u_ref[...], hn_ref[c], preferred_element_type=F32)
            return carry

        lax.fori_loop(0, nch, chunk, 0)

    @pl.when((j < n_blk) & (j % 2 == 0))
    def _even():
        steady(0, 1)

    @pl.when((j < n_blk) & (j % 2 == 1))
    def _odd():
        steady(1, 0)

    @pl.when(j == n_blk)
    def _finish():
        last = (N_KEYS // eb - 1) % 2
        for c in range(nch):
            acc = acc_ref[c] + jnp.dot(vt_ref[...], at_ref[last, c], preferred_element_type=F32)
            rows = slice(c * PEER_CHUNK, (c + 1) * PEER_CHUNK)
            y = acc.T + x_ref[rows, :]
            if final_norm:
                y = y * lax.rsqrt(jnp.mean(y * y, axis=-1, keepdims=True) + RMS_EPS) * gfin_ref[...]
            o_ref[rows, :] = y


def _peer_tiles(tokens):
    for tt in (512, 256):
        if tokens % tt == 0:
            return tt, 8
    raise ValueError(f"token count {tokens} must be a multiple of {PEER_CHUNK}")


def _peer(x, g, w_query, sub_keys, u_tab, v_tab, g_final, final_norm):
    tokens, dm = x.shape
    tt, eb = _peer_tiles(tokens)
    nlb = tt // LANES
    nch = tt // PEER_CHUNK
    eblk = eb * N_KEYS
    n_exp = u_tab.shape[0]
    assert n_exp == N_KEYS * N_KEYS and N_KEYS % eb == 0
    assert sub_keys.shape == (PEER_HEADS, 2, N_KEYS, PEER_HALF)
    n_blk = N_KEYS // eb
    wq_t = w_query.T.astype(BF16)
    keys = sub_keys.reshape(PEER_HEADS * 2, N_KEYS, PEER_HALF).astype(BF16)
    u_bf = u_tab.astype(BF16)
    vt_bf = v_tab.T.astype(BF16)
    row = lambda a: a.reshape(1, -1).astype(F32)
    grid = (tokens // tt, n_blk + 1)
    return pl.pallas_call(
        functools.partial(_peer_kernel, tt=tt, eb=eb, final_norm=final_norm),
        name="peer",
        grid=grid,
        in_specs=[
            pl.BlockSpec((tt, dm), lambda i, j: (i, 0)),
            _const_spec((1, dm)),
            _const_spec(wq_t.shape),
            _const_spec(keys.shape),
            _const_spec((eblk, dm)),
            pl.BlockSpec((eblk, dm), lambda i, j: (jnp.minimum(j + 1, n_blk - 1), 0)),
            pl.BlockSpec((dm, eblk), lambda i, j: (0, jnp.clip(j - 1, 0, n_blk - 1))),
            _const_spec((1, dm)),
        ],
        out_specs=pl.BlockSpec((tt, dm), lambda i, j: (i, 0)),
        out_shape=jax.ShapeDtypeStruct((tokens, dm), F32),
        scratch_shapes=[
            pltpu.VMEM((nch, dm, PEER_CHUNK), BF16),
            pltpu.VMEM((2 * PEER_HEADS, nlb, N_KEYS, LANES), F32),
            pltpu.VMEM((2, 3, PEER_TOPK, LANES), F32),
            pltpu.VMEM((PEER_HEADS, nlb, N_KEYS, LANES), F32),
            pltpu.VMEM((PEER_HEADS, nlb, N_KEYS, LANES), F32),
            pltpu.VMEM((PEER_HEADS, nlb, N_KEYS, LANES), F32),
            pltpu.VMEM((2, nch, eblk, PEER_CHUNK), F32),
            pltpu.VMEM((2, nch, eblk, PEER_CHUNK), BF16),
            pltpu.VMEM((nch, dm, PEER_CHUNK), F32),
        ],
        compiler_params=pltpu.CompilerParams(
            dimension_semantics=("arbitrary", "arbitrary"), vmem_limit_bytes=VMEM_LIMIT_BYTES),
    )(x, row(g), wq_t, keys, u_bf, u_bf, vt_bf, row(g_final))


def kernel(x, rms_mix_g, w_in, ret_gn_g, conv_dw_w, conv_dw_b, conv_ln_g, conv_ln_b, w_ret_proj, w_conv_proj,
           b_conv_proj, w_out, rms_ffn_g, w_query, peer_sub_keys, peer_u, peer_v, rms_final_g):
    bsz, seq, dm = x.shape
    depth = w_in.shape[0]
    for l in range(depth):
        x = _mixer(x, rms_mix_g[l], w_in[l], ret_gn_g[l], conv_dw_w[l], conv_dw_b[l], conv_ln_g[l],
                   conv_ln_b[l], w_ret_proj[l], w_conv_proj[l], b_conv_proj[l], w_out[l])
        x = _peer(x.reshape(bsz * seq, dm), rms_ffn_g[l], w_query[l], peer_sub_keys[l], peer_u[l], peer_v[l],
                  rms_final_g, final_norm=(l == depth - 1)).reshape(bsz, seq, dm)
    return x
```

```python
import functools

import numpy as np
import jax
import jax.numpy as jnp
from jax import lax
from jax.experimental import pallas as pl
from jax.experimental.pallas import tpu as pltpu

F32 = jnp.float32
BF16 = jnp.bfloat16

RET_HEADS = 4
RET_QK_DIM = 64
RET_V_DIM = 128
RET_QK = RET_HEADS * RET_QK_DIM
RET_WIDTH = RET_HEADS * RET_V_DIM
ROPE_BASE = 10000.0
CONV_CH = 512
CONV_K = 31
PEER_HEADS = 8
PEER_HALF = 128
N_KEYS = 128
PEER_TOPK = 16
RMS_EPS = 1e-6
LN_EPS = 1e-5
INV_SQRT2 = 0.7071067811865476

LANES = 128
SUBLANES = 8
VMEM_LIMIT_BYTES = 56 * 1024 * 1024
CONV_HIST = 32
CONV_ROWS = 32
PEER_CHUNK = 256
BUILD_ROWS = 32
BUILD_E1 = 4
NOT_SELECTED = 127.0


def _const_spec(shape):
    nd = len(shape)
    return pl.BlockSpec(shape, lambda *_: (0,) * nd, pipeline_mode=pl.Buffered(1))


def _swap_halves(t):
    half = RET_QK_DIM // 2
    parts = []
    for c in range(t.shape[1] // LANES):
        tc = t[:, c * LANES:(c + 1) * LANES]
        lane = lax.broadcasted_iota(jnp.int32, tc.shape, 1)
        first = (lane % RET_QK_DIM) < half
        nxt = pltpu.roll(tc, LANES - half, 1)
        prv = pltpu.roll(tc, half, 1)
        parts.append(jnp.where(first, nxt, prv))
    return jnp.concatenate(parts, axis=1)


def _mixer_kernel(x_ref, g_ref, win_ref, cos_ref, sin_ref, qd_ref, kd_ref, intra_ref, cd_ref, bd_ref,
                  hm_ref, gn_ref, dww_ref, dwb_ref, lng_ref, lnb_ref, wret_ref, wconv_ref, bconv_ref,
                  wout_ref, o_ref, state_ref, cbuf_ref, conv_ref, shift_ref, *, ts):
    @pl.when(pl.program_id(1) == 0)
    def _():
        state_ref[...] = jnp.zeros_like(state_ref)
        cbuf_ref[0:CONV_HIST, :] = jnp.zeros((CONV_HIST, CONV_CH), F32)

    x = x_ref[...]
    h = (x * lax.rsqrt(jnp.mean(x * x, axis=-1, keepdims=True) + RMS_EPS) * g_ref[...]).astype(BF16)

    c0 = 2 * RET_QK + RET_WIDTH
    qkv = jnp.dot(h, win_ref[:, 0:c0], preferred_element_type=F32)
    q = qkv[:, 0:RET_QK]
    k = qkv[:, RET_QK:2 * RET_QK]
    vb = qkv[:, 2 * RET_QK:c0].astype(BF16)
    cos = cos_ref[...]
    sin = sin_ref[...]
    qr = q * cos + _swap_halves(q) * sin
    kr = (k * cos + _swap_halves(k) * sin) * (RET_QK_DIM ** -0.5)
    qb = qr.astype(BF16)
    kbt = kr.T.astype(BF16)
    inner = []
    for hh in range(RET_HEADS):
        qm = (qr * hm_ref[hh:hh + 1, :]).astype(BF16)
        sc = jnp.dot(qm, kbt, preferred_element_type=F32)
        sc = sc * intra_ref[hh]
        inner.append(jnp.dot(sc.astype(BF16), vb[:, hh * RET_V_DIM:(hh + 1) * RET_V_DIM],
                             preferred_element_type=F32))
    st = state_ref[...]
    cross = jnp.dot(qb, st.astype(BF16), preferred_element_type=F32) * qd_ref[...]
    o = jnp.concatenate(inner, axis=1) + cross
    kv = lax.dot_general((kr * kd_ref[...]).astype(BF16), vb, (((0,), (0,)), ((), ())),
                         preferred_element_type=F32)
    state_ref[...] = st * cd_ref[...] + kv * bd_ref[...]

    normed = []
    for hh in range(RET_HEADS):
        oh = o[:, hh * RET_V_DIM:(hh + 1) * RET_V_DIM]
        d = oh - jnp.mean(oh, axis=-1, keepdims=True)
        normed.append(d * lax.rsqrt(jnp.mean(d * d, axis=-1, keepdims=True) + LN_EPS))
    on = jnp.concatenate(normed, axis=1) * gn_ref[...]

    c1 = c0 + RET_WIDTH + 2 * CONV_CH
    p2 = jnp.dot(h, win_ref[:, c0:c1], preferred_element_type=F32)
    g_ret = p2[:, 0:RET_WIDTH]
    yret = (g_ret * jax.nn.sigmoid(g_ret) * on).astype(BF16)
    a_out = jnp.dot(yret, wret_ref[...], preferred_element_type=F32)

    u_val = p2[:, RET_WIDTH:RET_WIDTH + CONV_CH]
    u_gate = p2[:, RET_WIDTH + CONV_CH:RET_WIDTH + 2 * CONV_CH]
    cbuf_ref[CONV_HIST:CONV_HIST + ts, :] = u_val * jax.nn.sigmoid(u_gate)
    base = CONV_HIST - (CONV_K - 1)
    groups = [[o for o in range(base, base + CONV_K) if o % SUBLANES == r] for r in range(SUBLANES)]
    for r in range(1, SUBLANES):
        offs = groups[r]
        n_rows = ts + offs[-1] - offs[0]
        shift_ref[r, 0:n_rows, :] = cbuf_ref[pl.ds(offs[0], n_rows), :]
    tiles = (CONV_ROWS // SUBLANES, SUBLANES, CONV_CH)
    for rc in range(ts // CONV_ROWS):
        acc = jnp.broadcast_to(dwb_ref[...][None], tiles)
        for r in range(SUBLANES):
            for o in groups[r]:
                if r == 0:
                    tap = cbuf_ref[pl.ds(rc * CONV_ROWS + o, CONV_ROWS), :]
                else:
                    tap = shift_ref[r, pl.ds(rc * CONV_ROWS + o - groups[r][0], CONV_ROWS), :]
                acc = acc + dww_ref[o - base][None] * tap.reshape(tiles)
        conv_ref[rc * CONV_ROWS:(rc + 1) * CONV_ROWS, :] = acc.reshape(CONV_ROWS, CONV_CH)
    cbuf_ref[0:CONV_HIST, :] = cbuf_ref[ts:ts + CONV_HIST, :]
    cv = conv_ref[...]
    d = cv - jnp.mean(cv, axis=-1, keepdims=True)
    ln = d * lax.rsqrt(jnp.mean(d * d, axis=-1, keepdims=True) + LN_EPS) * lng_ref[...] + lnb_ref[...]
    yconv = (ln * jax.nn.sigmoid(ln)).astype(BF16)
    b_out = jnp.dot(yconv, wconv_ref[...], preferred_element_type=F32) + bconv_ref[...]

    p3 = jnp.dot(h, win_ref[:, c1:], preferred_element_type=F32)
    dm = x.shape[1]
    mixed = jax.nn.sigmoid(p3[:, 0:dm]) * a_out + jax.nn.sigmoid(p3[:, dm:2 * dm]) * b_out
    o_ref[...] = x + jnp.dot(mixed.astype(BF16), wout_ref[...], preferred_element_type=F32)


def _mixer_tables(seq, ts):
    half = RET_QK_DIM // 2
    inv = 1.0 / (ROPE_BASE ** (np.arange(half, dtype=np.float64) / half))
    ang = np.arange(seq, dtype=np.float64)[:, None] * inv[None, :]
    cos = np.tile(np.concatenate([np.cos(ang), np.cos(ang)], axis=1), (1, RET_HEADS))
    sin = np.tile(np.concatenate([-np.sin(ang), np.sin(ang)], axis=1), (1, RET_HEADS))
    log_gamma = np.log1p(-(2.0 ** (-5.0 - np.arange(RET_HEADS, dtype=np.float64))))
    n = np.arange(ts, dtype=np.float64)
    diff = n[:, None] - n[None, :]
    intra = np.where(diff[None] >= 0, np.exp(log_gamma[:, None, None] * np.maximum(diff, 0.0)[None]), 0.0)
    q_decay = np.exp(log_gamma[:, None] * (n + 1.0)[None])
    k_decay = np.exp(log_gamma[:, None] * (ts - 1.0 - n)[None])
    chunk_decay = np.exp(log_gamma * ts)
    qd = np.repeat(q_decay.T, RET_V_DIM, axis=1)
    kd = np.repeat(k_decay.T, RET_QK_DIM, axis=1)
    cd = np.repeat(chunk_decay, RET_V_DIM)[None, :]
    row_head = np.arange(RET_QK) // RET_QK_DIM
    col_head = np.arange(RET_WIDTH) // RET_V_DIM
    bd = (row_head[:, None] == col_head[None, :]).astype(np.float64)
    hm = (np.arange(RET_HEADS)[:, None] == row_head[None, :]).astype(np.float64)
    as32 = lambda a: jnp.asarray(a, dtype=F32)
    return tuple(as32(a) for a in (cos, sin, qd, kd, intra, cd, bd, hm))


def _mixer_tile(seq):
    for ts in (256, 128, 64):
        if seq % ts == 0:
            return ts
    raise ValueError(f"sequence length {seq} must be a multiple of {CONV_ROWS}")


def _mixer(x, g, w_in, gn_g, dw_w, dw_b, ln_g, ln_b, w_ret, w_conv, b_conv, w_out):
    bsz, seq, dm = x.shape
    ts = _mixer_tile(seq)
    cos, sin, qd, kd, intra, cd, bd, hm = _mixer_tables(seq, ts)
    row = lambda a: a.reshape(1, -1).astype(F32)
    in_cols = w_in.shape[1]
    grid = (bsz, seq // ts)
    in_specs = [
        pl.BlockSpec((None, ts, dm), lambda b, j: (b, j, 0)),
        _const_spec((1, dm)),
        _const_spec((dm, in_cols)),
        pl.BlockSpec((ts, RET_QK), lambda b, j: (j, 0)),
        pl.BlockSpec((ts, RET_QK), lambda b, j: (j, 0)),
        _const_spec((ts, RET_WIDTH)),
        _const_spec((ts, RET_QK)),
        _const_spec((RET_HEADS, ts, ts)),
        _const_spec((1, RET_WIDTH)),
        _const_spec((RET_QK, RET_WIDTH)),
        _const_spec((RET_HEADS, RET_QK)),
        _const_spec((1, RET_WIDTH)),
        _const_spec((CONV_K, SUBLANES, CONV_CH)),
        _const_spec((SUBLANES, CONV_CH)),
        _const_spec((1, CONV_CH)),
        _const_spec((1, CONV_CH)),
        _const_spec((RET_WIDTH, dm)),
        _const_spec((CONV_CH, dm)),
        _const_spec((1, dm)),
        _const_spec((dm, dm)),
    ]
    return pl.pallas_call(
        functools.partial(_mixer_kernel, ts=ts),
        name="mixer",
        grid=grid,
        in_specs=in_specs,
        out_specs=pl.BlockSpec((None, ts, dm), lambda b, j: (b, j, 0)),
        out_shape=jax.ShapeDtypeStruct((bsz, seq, dm), F32),
        scratch_shapes=[
            pltpu.VMEM((RET_QK, RET_WIDTH), F32),
            pltpu.VMEM((CONV_HIST + ts, CONV_CH), F32),
            pltpu.VMEM((ts, CONV_CH), F32),
            pltpu.VMEM((SUBLANES, ts + CONV_HIST - SUBLANES, CONV_CH), F32),
        ],
        compiler_params=pltpu.CompilerParams(
            dimension_semantics=("arbitrary", "arbitrary"), vmem_limit_bytes=VMEM_LIMIT_BYTES),
    )(x, row(g), w_in.astype(BF16), cos, sin, qd, kd, intra, cd, bd, hm, row(gn_g),
      jnp.broadcast_to(dw_w.astype(F32)[:, None, :], (CONV_K, SUBLANES, CONV_CH)),
      jnp.broadcast_to(row(dw_b), (SUBLANES, CONV_CH)), row(ln_g), row(ln_b), w_ret.astype(BF16), w_conv.astype(BF16), row(b_conv),
      w_out.astype(BF16))


def _zip_steps(*steppers):
    results = [None] * len(steppers)
    live = dict(enumerate(steppers))
    while live:
        for idx in list(live):
            try:
                next(live[idx])
            except StopIteration as done:
                results[idx] = done.value
                del live[idx]
        if live:
            yield
    return results


def _interleave(*steppers):
    zipped = _zip_steps(*steppers)
    try:
        while True:
            next(zipped)
    except StopIteration as done:
        return done.value


def _top16(s, row_id, break_ties, want_rank=True):
    cur = s
    rank = jnp.full(s.shape, NOT_SELECTED, F32) if want_rank else None
    vals = []
    for r in range(PEER_TOPK):
        m = jnp.max(cur, axis=0, keepdims=True)
        if break_ties:
            first = jnp.min(jnp.where(cur == m, row_id, 1e9), axis=0, keepdims=True)
            hit = row_id == first
        else:
            hit = cur == m
        if want_rank:
            rank = jnp.where(hit, float(r), rank)
        cur = jnp.where(hit, -jnp.inf, cur)
        vals.append(m)
        yield
    return jnp.concatenate(vals, axis=0), rank


def _merge_exchange_network(n):
    pairs = []
    t = n.bit_length() - 1
    p = 1 << (t - 1)
    while p > 0:
        q, r, d = 1 << (t - 1), 0, p
        while d > 0:
            pairs += [(i, i + d) for i in range(n - d) if (i & p) == r]
            d, q, r = q - p, q >> 1, p
        p >>= 1
    return pairs


def _top16_values(rows):
    k = PEER_TOPK
    n_real = len(rows)
    assert n_real <= k
    x = list(rows) + [None] * (k - n_real)
    for n, (i, j) in enumerate(_merge_exchange_network(k)):
        if x[i] is None:
            x[i], x[j] = x[j], None
        elif x[j] is not None:
            x[i], x[j] = jnp.maximum(x[i], x[j]), jnp.minimum(x[i], x[j])
        if n % 8 == 7:
            yield
    x = [jnp.full((SUBLANES, LANES), -jnp.inf, F32) if xi is None else xi for xi in x]
    shift = SUBLANES // 2
    while shift >= 1:
        x = [jnp.maximum(x[i], pltpu.roll(x[k - 1 - i], shift, 0)) for i in range(k)]
        yield
        stride = k // 2
        while stride >= 1:
            for i in range(k):
                if (i & stride) == 0:
                    x[i], x[i + stride] = jnp.maximum(x[i], x[i + stride]), jnp.minimum(x[i], x[i + stride])
            stride //= 2
            yield
        shift //= 2
    dup = jnp.zeros((SUBLANES, LANES), F32)
    for i in range(k - 1):
        dup = jnp.where(x[i] == x[i + 1], 1.0, dup)
    row = lax.broadcasted_iota(jnp.int32, (SUBLANES, LANES), 0)
    halves = []
    for h0 in (0, SUBLANES):
        v = x[h0]
        for i in range(1, SUBLANES):
            v = jnp.where(row == i, x[h0 + i], v)
        halves.append(v)
    return jnp.concatenate(halves, axis=0), dup


def _staircase(v1, v2, e1x, e2x, break_ties):
    k = PEER_TOPK
    row = lax.broadcasted_iota(jnp.int32, (SUBLANES, LANES), 0).astype(F32)
    ninf = jnp.full((SUBLANES, LANES), -jnp.inf, F32)
    cand, pos, wgt = [], [], []
    for b0 in (0, SUBLANES):
        cand.append(v1[0:1] + v2[b0:b0 + SUBLANES]); pos.append(row + float(b0))
        wgt.append(e1x[0:1] * e2x[b0:b0 + SUBLANES])
    cand.append(v1[1:2] + v2[0:SUBLANES]); pos.append(row + float(k)); wgt.append(e1x[1:2] * e2x[0:SUBLANES])
    cand.append(v1[SUBLANES:k] + v2[0:1]); pos.append((row + float(SUBLANES)) * float(k))
    wgt.append(e1x[SUBLANES:k] * e2x[0:1])
    n_col = 5
    for b in range(n_col):
        a_max = float(k // (b + 1) - 1)
        ok = (row >= 2.0) & (row <= a_max)
        cand.append(jnp.where(ok, v1[0:SUBLANES] + v2[b:b + 1], ninf))
        pos.append(jnp.where(ok, row * float(k) + float(b), -1.0))
        wgt.append(e1x[0:SUBLANES] * e2x[b:b + 1])
    ng = len(cand)
    sel = [jnp.zeros((SUBLANES, LANES), F32) for _ in range(ng)]
    if not break_ties:
        best, _ = yield from _top16_values(cand)
        sel = [jnp.where(c >= best[k - 1:k], 1.0, 0.0) for c in cand]
    for _ in range(k if break_ties else 0):
        m = cand[0]
        for g in range(1, ng):
            m = jnp.maximum(m, cand[g])
        m = jnp.max(m, axis=0, keepdims=True)
        fp = jnp.where(cand[0] == m, pos[0], 1e9)
        for g in range(1, ng):
            fp = jnp.minimum(fp, jnp.where(cand[g] == m, pos[g], 1e9))
        fp = jnp.min(fp, axis=0, keepdims=True)
        for g in range(ng):
            hit = pos[g] == fp
            sel[g] = jnp.where(hit, 1.0, sel[g])
            cand[g] = jnp.where(hit, -jnp.inf, cand[g])
        yield
    b0cnt = jnp.sum(sel[0] + sel[1], axis=0, keepdims=True)
    b1cnt = jnp.sum(sel[2], axis=0, keepdims=True)
    low = sel[4]
    for g in range(5, ng):
        low = low + sel[g]
    low = low + jnp.where(row == 0.0, b0cnt, 0.0) + jnp.where(row == 1.0, b1cnt, 0.0)
    counts = jnp.concatenate([low, sel[3]], axis=0)
    zt = sel[0] * wgt[0]
    nsel = sel[0]
    for g in range(1, ng):
        zt = zt + sel[g] * wgt[g]
        nsel = nsel + sel[g]
    return counts, jnp.sum(zt, axis=0, keepdims=True), jnp.sum(nsel, axis=0, keepdims=True)


def _peer_kernel(x_ref, g_ref, wq_ref, keys_ref, u0_ref, u_ref, v_ref, gfin_ref, o_ref,
                 hn_ref, s_ref, vbuf_ref, w2_ref, thr_ref, w1_ref, st_ref, at_ref, acc_ref,
                 *, tt, eb, final_norm):
    j = pl.program_id(1)
    n_blk = pl.num_programs(1) - 1
    nch = tt // PEER_CHUNK
    lpc = PEER_CHUNK // LANES
    qd = 2 * PEER_HALF
    tn_dims = (((0,), (0,)), ((), ()))

    @pl.when(j == 0)
    def _routing():
        x = x_ref[...]
        hn = x * lax.rsqrt(jnp.mean(x * x, axis=-1, keepdims=True) + RMS_EPS) * g_ref[...]
        for c in range(nch):
            hn_t = hn[c * PEER_CHUNK:(c + 1) * PEER_CHUNK, :].T
            hq_c = hn_t.astype(BF16)
            hn_c = (hn_t * INV_SQRT2).astype(BF16)
            hn_ref[c] = hn_c
            acc_ref[c] = jnp.zeros(acc_ref.shape[1:], F32)
            at_ref[1, c] = jnp.zeros(at_ref.shape[2:], BF16)
            st_ref[0, c] = jnp.dot(u0_ref[...], hn_c, preferred_element_type=F32)
            for hh in range(PEER_HEADS):
                qt = jnp.dot(wq_ref[hh * qd:(hh + 1) * qd, :], hq_c, preferred_element_type=F32).astype(BF16)
                for p in range(2):
                    sc = jnp.dot(keys_ref[2 * hh + p], qt[p * PEER_HALF:(p + 1) * PEER_HALF, :],
                                 preferred_element_type=F32)
                    for l in range(lpc):
                        s_ref[2 * hh + p, c * lpc + l] = sc[:, l * LANES:(l + 1) * LANES]

        row_id = lax.broadcasted_iota(jnp.int32, (N_KEYS, LANES), 0).astype(F32)

        n_route = PEER_HEADS * nch * lpc
        topk = float(PEER_TOPK)

        def scores(blk):
            hh = blk // (nch * lpc)
            lb = blk % (nch * lpc)
            return hh, lb, s_ref[2 * hh, lb], s_ref[2 * hh + 1, lb]

        def dense_out(hh, lb, s1, s2, v1, v2, z, thr):
            thr_ref[hh, lb] = thr
            w1_ref[hh, lb] = jnp.exp(s1 - v1[0:1]) * (INV_SQRT2 / z)
            w2_ref[hh, lb] = jnp.exp(s2 - v2[0:1])

        def fast_a(blk, slot):
            hh, lb, s1, s2 = scores(blk)
            groups = lambda s: [s[i * SUBLANES:(i + 1) * SUBLANES, :] for i in range(N_KEYS // SUBLANES)]
            (v1, dup1), (v2, dup2) = yield from _zip_steps(_top16_values(groups(s1)), _top16_values(groups(s2)))
            last = slice(PEER_TOPK - 1, PEER_TOPK)
            in_top = jnp.where(s1 >= v1[last], 1.0, 0.0) + jnp.where(s2 >= v2[last], 1.0, 0.0)
            vbuf_ref[slot, 0] = v1
            vbuf_ref[slot, 1] = v2
            vbuf_ref[slot, 2, 0:1, :] = (jnp.sum(in_top, axis=0, keepdims=True)
                                         + jnp.max(dup1 + dup2, axis=0, keepdims=True))

        def fast_b(blk, slot):
            hh, lb, s1, s2 = scores(blk)
            v1 = vbuf_ref[slot, 0]
            v2 = vbuf_ref[slot, 1]
            counts, z, nsel = yield from _staircase(v1, v2, jnp.exp(v1 - v1[0:1]), jnp.exp(v2 - v2[0:1]), False)
            tval = jnp.full((PEER_TOPK, LANES), jnp.inf, F32)
            for b in range(PEER_TOPK):
                tval = jnp.where(counts == float(b + 1), v2[b:b + 1], tval)
            thr = jnp.full((N_KEYS, LANES), jnp.inf, F32)
            for a in range(SUBLANES):
                thr = jnp.where(s1 == v1[a:a + 1], tval[a:a + 1], thr)
            lo = jnp.min(jnp.where(counts[SUBLANES:] > 0.0, v1[SUBLANES:], jnp.inf), axis=0, keepdims=True)
            thr = jnp.where(s1 < v1[SUBLANES - 1:SUBLANES], jnp.where(s1 >= lo, v2[0:1], thr), thr)
            dense_out(hh, lb, s1, s2, v1, v2, z, thr)
            return vbuf_ref[slot, 2, 0:1, :] + nsel

        def exact(blk):
            hh, lb, s1, s2 = scores(blk)
            (v1, rank1), (v2, rank2) = _interleave(_top16(s1, row_id, True), _top16(s2, row_id, True))
            (counts, z, _), = _interleave(
                _staircase(v1, v2, jnp.exp(v1 - v1[0:1]), jnp.exp(v2 - v2[0:1]), True))
            bcnt = jnp.zeros((N_KEYS, LANES), F32)
            for a in range(PEER_TOPK):
                bcnt = jnp.where(rank1 == float(a), counts[a:a + 1], bcnt)
            dense_out(hh, lb, s1, s2, v1, v2, z, jnp.where(bcnt > 0.0, 1.0 - bcnt, jnp.inf))
            s_ref[2 * hh + 1, lb] = -rank2

        _interleave(fast_a(0, 0))

        def route(i, carry):
            slot = i % 2
            _, picked = _interleave(fast_a(jnp.minimum(i + 1, n_route - 1), 1 - slot), fast_b(i, slot))
            tied = jnp.max(jnp.where(picked == 3.0 * topk, 0.0, 1.0))

            @pl.when(tied > 0.0)
            def _():
                exact(i)

            return carry

        lax.fori_loop(0, n_route, route, 0)

    def steady(cur, nxt):
        def chunk(c, carry):
            acc_ref[c] += lax.dot_general(v_ref[...], at_ref[nxt, c], tn_dims,
                                          preferred_element_type=F32)
            for e0 in range(0, eb, BUILD_E1):
                for l in range(lpc):
                    lb = c * lpc + l
                    cols = slice(l * LANES, (l + 1) * LANES)
                    for g in range(N_KEYS // BUILD_ROWS):
                        grow = slice(g * BUILD_ROWS, (g + 1) * BUILD_ROWS)
                        gates = [jnp.zeros((BUILD_ROWS, LANES), F32) for _ in range(BUILD_E1)]
                        for hh in range(PEER_HEADS):
                            key = s_ref[2 * hh + 1, lb, grow, :]
                            w2 = w2_ref[hh, lb, grow, :]
                            for k in range(BUILD_E1):
                                e1 = j * eb + e0 + k
                                thr = thr_ref[hh, lb, pl.ds(e1, 1), :]
                                w1 = w1_ref[hh, lb, pl.ds(e1, 1), :]
                                gates[k] = gates[k] + jnp.where(key >= thr, w2, 0.0) * w1
                        for k in range(BUILD_E1):
                            rows = slice((e0 + k) * N_KEYS + g * BUILD_ROWS, (e0 + k) * N_KEYS + (g + 1) * BUILD_ROWS)
                            xs = st_ref[cur, c, rows, cols]
                            ge = xs * (1.0 + lax.erf(xs))
                            at_ref[cur, c, rows, cols] = (ge * gates[k]).astype(BF16)
            st_ref[nxt, c] = jnp.dot(u_ref[...], hn_ref[c], preferred_element_type=F32)
            return carry

        lax.fori_loop(0, nch, chunk, 0)

    @pl.when((j < n_blk) & (j % 2 == 0))
    def _even():
        steady(0, 1)

    @pl.when((j < n_blk) & (j % 2 == 1))
    def _odd():
        steady(1, 0)

    @pl.when(j == n_blk)
    def _finish():
        last = (N_KEYS // eb - 1) % 2
        for c in range(nch):
            acc = acc_ref[c] + lax.dot_general(v_ref[...], at_ref[last, c], tn_dims, preferred_element_type=F32)
            rows = slice(c * PEER_CHUNK, (c + 1) * PEER_CHUNK)
            y = acc.T + x_ref[rows, :]
            if final_norm:
                y = y * lax.rsqrt(jnp.mean(y * y, axis=-1, keepdims=True) + RMS_EPS) * gfin_ref[...]
            o_ref[rows, :] = y


def _peer_tiles(tokens):
    for tt in (512, 256):
        if tokens % tt == 0:
            return tt, 8
    raise ValueError(f"token count {tokens} must be a multiple of {PEER_CHUNK}")


def _peer(x, g, w_query, sub_keys, u_tab, v_tab, g_final, final_norm):
    tokens, dm = x.shape
    tt, eb = _peer_tiles(tokens)
    nlb = tt // LANES
    nch = tt // PEER_CHUNK
    eblk = eb * N_KEYS
    n_exp = u_tab.shape[0]
    assert n_exp == N_KEYS * N_KEYS and N_KEYS % eb == 0
    assert sub_keys.shape == (PEER_HEADS, 2, N_KEYS, PEER_HALF)
    n_blk = N_KEYS // eb
    wq_t = w_query.T.astype(BF16)
    keys = sub_keys.reshape(PEER_HEADS * 2, N_KEYS, PEER_HALF).astype(BF16)
    u_bf = u_tab.astype(BF16)
    v_bf = v_tab.astype(BF16)
    row = lambda a: a.reshape(1, -1).astype(F32)
    grid = (tokens // tt, n_blk + 1)
    return pl.pallas_call(
        functools.partial(_peer_kernel, tt=tt, eb=eb, final_norm=final_norm),
        name="peer",
        grid=grid,
        in_specs=[
            pl.BlockSpec((tt, dm), lambda i, j: (i, 0)),
            _const_spec((1, dm)),
            _const_spec(wq_t.shape),
            _const_spec(keys.shape),
            _const_spec((eblk, dm)),
            pl.BlockSpec((eblk, dm), lambda i, j: (jnp.minimum(j + 1, n_blk - 1), 0)),
            pl.BlockSpec((eblk, dm), lambda i, j: (jnp.clip(j - 1, 0, n_blk - 1), 0)),
            _const_spec((1, dm)),
        ],
        out_specs=pl.BlockSpec((tt, dm), lambda i, j: (i, 0)),
        out_shape=jax.ShapeDtypeStruct((tokens, dm), F32),
        scratch_shapes=[
            pltpu.VMEM((nch, dm, PEER_CHUNK), BF16),
            pltpu.VMEM((2 * PEER_HEADS, nlb, N_KEYS, LANES), F32),
            pltpu.VMEM((2, 3, PEER_TOPK, LANES), F32),
            pltpu.VMEM((PEER_HEADS, nlb, N_KEYS, LANES), F32),
            pltpu.VMEM((PEER_HEADS, nlb, N_KEYS, LANES), F32),
            pltpu.VMEM((PEER_HEADS, nlb, N_KEYS, LANES), F32),
            pltpu.VMEM((2, nch, eblk, PEER_CHUNK), F32),
            pltpu.VMEM((2, nch, eblk, PEER_CHUNK), BF16),
            pltpu.VMEM((nch, dm, PEER_CHUNK), F32),
        ],
        compiler_params=pltpu.CompilerParams(
            dimension_semantics=("arbitrary", "arbitrary"), vmem_limit_bytes=VMEM_LIMIT_BYTES),
    )(x, row(g), wq_t, keys, u_bf, u_bf, v_bf, row(g_final))


def kernel(x, rms_mix_g, w_in, ret_gn_g, conv_dw_w, conv_dw_b, conv_ln_g, conv_ln_b, w_ret_proj, w_conv_proj,
           b_conv_proj, w_out, rms_ffn_g, w_query, peer_sub_keys, peer_u, peer_v, rms_final_g):
    bsz, seq, dm = x.shape
    depth = w_in.shape[0]
    for l in range(depth):
        x = _mixer(x, rms_mix_g[l], w_in[l], ret_gn_g[l], conv_dw_w[l], conv_dw_b[l], conv_ln_g[l],
                   conv_ln_b[l], w_ret_proj[l], w_conv_proj[l], b_conv_proj[l], w_out[l])
        x = _peer(x.reshape(bsz * seq, dm), rms_ffn_g[l], w_query[l], peer_sub_keys[l], peer_u[l], peer_v[l],
                  rms_final_g, final_norm=(l == depth - 1)).reshape(bsz, seq, dm)
    return x
```
